```python
import jax, jax.numpy as jnp
from jax import lax
import numpy as np

D_MODEL = 1024
BATCH = 2
SEQ = 16384
DEPTH = 2

HEAD_DIM = D_MODEL // 16
RET_HEADS = 4
RET_DK = HEAD_DIM
RET_DV = HEAD_DIM
RET_CHUNK = 128
MLA_HEADS = 4
MLA_NOPE = 2 * HEAD_DIM
MLA_ROPE = HEAD_DIM
MLA_DV = 2 * HEAD_DIM
MLA_Q_RANK = 3 * D_MODEL // 8
MLA_KV_RANK = D_MODEL // 4
MLA_BLOCK = 128
GLA_HEADS = 4
GLA_DK = HEAD_DIM
GLA_DV = HEAD_DIM
GLA_GATE_RANK = 16
GLA_TAU = 16.0
GLA_CHUNK = 128
D_MIX = RET_HEADS * RET_DV + MLA_HEADS * MLA_DV + GLA_HEADS * GLA_DV
IN_SPLITS = (
    RET_HEADS * RET_DK, RET_HEADS * RET_DK, RET_HEADS * RET_DV, RET_HEADS * RET_DV,
    MLA_Q_RANK, MLA_KV_RANK, MLA_ROPE,
    GLA_HEADS * GLA_DK, GLA_HEADS * GLA_DK, GLA_HEADS * GLA_DV, GLA_GATE_RANK,
    GLA_HEADS * GLA_DV,
)
IN_COLS = sum(IN_SPLITS)
D_FF = ((8 * D_MODEL // 3 + 127) // 128) * 128
ROPE_THETA = 10000.0
EPS = 1e-6

kernel_name = "hybrid_retention_mla_gla_macaron"


def rms_norm(x, g):
    xf = x.astype(jnp.float32)
    y = xf * lax.rsqrt(jnp.mean(xf * xf, axis=-1, keepdims=True) + EPS)
    return (y * g.astype(jnp.float32)).astype(x.dtype)


def rope_tables(positions, dim):
    inv = ROPE_THETA ** (-jnp.arange(0, dim, 2, dtype=jnp.float32) / dim)
    ang = positions.astype(jnp.float32)[..., None] * inv
    return jnp.cos(ang), jnp.sin(ang)


def apply_rope(x, cos, sin):
    x1, x2 = jnp.split(x, 2, axis=-1)
    c = cos[:, :, None, :].astype(x.dtype)
    s = sin[:, :, None, :].astype(x.dtype)
    return jnp.concatenate([x1 * c - x2 * s, x1 * s + x2 * c], axis=-1)


def swiglu(x, w_gate_up, w_down):
    g, u = jnp.split(x @ w_gate_up, 2, axis=-1)
    return (jax.nn.silu(g) * u) @ w_down


def to_chunks(x, c):
    b, s, h, d = x.shape
    return x.reshape(b, s // c, c, h, d).transpose(1, 0, 3, 2, 4)


def from_chunks(o):
    n, b, h, c, d = o.shape
    return o.transpose(1, 0, 3, 2, 4).reshape(b, n * c, h * d)


def retention(q, k, v):
    out_dtype = v.dtype
    b, s, h, dk = q.shape
    dv = v.shape[-1]
    c = RET_CHUNK
    q = q.astype(jnp.float32)
    k = k.astype(jnp.float32) * (dk ** -0.5)
    v = v.astype(jnp.float32)
    log_g = jnp.log1p(-jnp.power(2.0, -5.0 - jnp.arange(h, dtype=jnp.float32)))
    idx = jnp.arange(c, dtype=jnp.float32)
    diff = idx[:, None] - idx[None, :]
    intra = jnp.where(diff >= 0, jnp.exp(log_g[:, None, None] * jnp.maximum(diff, 0.0)), 0.0)
    xi = jnp.exp(log_g[:, None] * (idx + 1.0))[:, :, None]
    zeta = jnp.exp(log_g[:, None] * (c - 1.0 - idx))[:, :, None]
    chunk_decay = jnp.exp(log_g * c)[:, None, None]

    def step(state, inp):
        qi, ki, vi = inp
        sc = jnp.einsum('bhtd,bhsd->bhts', qi, ki) * intra
        o = jnp.einsum('bhts,bhsv->bhtv', sc, vi) + jnp.einsum('bhtd,bhdv->bhtv', qi, state) * xi
        state = state * chunk_decay + jnp.einsum('bhsd,bhsv->bhdv', ki, vi * zeta)
        return state, o

    state0 = jnp.zeros((b, h, dk, dv), jnp.float32)
    _, o = lax.scan(step, state0, (to_chunks(q, c), to_chunks(k, c), to_chunks(v, c)))
    return from_chunks(o).astype(out_dtype)


def gated_linear_attention(q, k, v, log_a):
    out_dtype = v.dtype
    b, s, h, dk = q.shape
    dv = v.shape[-1]
    c = GLA_CHUNK
    q = q.astype(jnp.float32) * (dk ** -0.5)
    k = k.astype(jnp.float32)
    v = v.astype(jnp.float32)
    log_a = log_a.astype(jnp.float32)
    mask = jnp.tril(jnp.ones((c, c), dtype=bool))[:, :, None]

    def step(state, inp):
        qi, ki, vi, ai = inp
        bcum = jnp.cumsum(ai, axis=2)
        rel = bcum[:, :, :, None, :] - bcum[:, :, None, :, :]
        dec = jnp.exp(jnp.where(mask, rel, -jnp.inf))
        sc = jnp.sum(qi[:, :, :, None, :] * ki[:, :, None, :, :] * dec, axis=-1)
        o = jnp.einsum('bhts,bhsv->bhtv', sc, vi) + jnp.einsum('bhtd,bhdv->bhtv', qi * jnp.exp(bcum), state)
        b_last = bcum[:, :, -1:, :]
        state = jnp.exp(b_last[:, :, 0, :])[..., None] * state + jnp.einsum(
            'bhsd,bhsv->bhdv', ki * jnp.exp(b_last - bcum), vi)
        return state, o

    state0 = jnp.zeros((b, h, dk, dv), jnp.float32)
    _, o = lax.scan(step, state0, (to_chunks(q, c), to_chunks(k, c), to_chunks(v, c), to_chunks(log_a, c)))
    return from_chunks(o).astype(out_dtype)


def causal_block_attention(q, k, v):
    b, s, h, d = q.shape
    dv = v.shape[-1]
    nb = s // MLA_BLOCK
    scale = d ** -0.5
    qb = q.reshape(b, nb, MLA_BLOCK, h, d).transpose(1, 0, 3, 2, 4)
    kh = k.transpose(0, 2, 1, 3)
    vh = v.transpose(0, 2, 1, 3)
    kpos = jnp.arange(s)

    def one_block(args):
        i, qi = args
        sc = jnp.einsum('bhqd,bhkd->bhqk', qi, kh).astype(jnp.float32) * scale
        qpos = i * MLA_BLOCK + jnp.arange(MLA_BLOCK)
        sc = jnp.where(kpos[None, :] <= qpos[:, None], sc, -jnp.inf)
        p = jax.nn.softmax(sc, axis=-1).astype(vh.dtype)
        return jnp.einsum('bhqk,bhkv->bhqv', p, vh)

    o = lax.map(one_block, (jnp.arange(nb), qb))
    return o.transpose(1, 0, 3, 2, 4).reshape(b, s, h * dv)


def hybrid_mixer(h, cos, sin, w_in, ret_out_norm,
                 mla_q_norm, mla_w_uq, mla_kv_norm, mla_w_ukv,
                 mla_q_nope_norm, mla_q_rope_norm, mla_k_nope_norm, mla_k_rope_norm,
                 gla_w_gate_up, gla_gate_bias, gla_out_norm, w_out):
    b, s, _ = h.shape
    proj = h @ w_in
    points = []
    acc = 0
    for sz in IN_SPLITS[:-1]:
        acc += sz
        points.append(acc)
    (ret_q, ret_k, ret_v, ret_g, mla_cq, mla_ckv, mla_kr,
     gla_q, gla_k, gla_v, gla_a_low, gla_r) = jnp.split(proj, points, axis=-1)

    rq = apply_rope(ret_q.reshape(b, s, RET_HEADS, RET_DK), cos, sin)
    rk = apply_rope(ret_k.reshape(b, s, RET_HEADS, RET_DK), cos, sin)
    rv = ret_v.reshape(b, s, RET_HEADS, RET_DV)
    ry = retention(rq, rk, rv).reshape(b, s, RET_HEADS, RET_DV)
    ry = rms_norm(ry, ret_out_norm.reshape(RET_HEADS, RET_DV)).reshape(b, s, RET_HEADS * RET_DV)
    ret_out = jax.nn.silu(ret_g) * ry

    cq = rms_norm(mla_cq, mla_q_norm)
    qf = (cq @ mla_w_uq).reshape(b, s, MLA_HEADS, MLA_NOPE + MLA_ROPE)
    ckv = rms_norm(mla_ckv, mla_kv_norm)
    kv = (ckv @ mla_w_ukv).reshape(b, s, MLA_HEADS, MLA_NOPE + MLA_DV)
    q_nope = rms_norm(qf[..., :MLA_NOPE], mla_q_nope_norm)
    q_rope = apply_rope(rms_norm(qf[..., MLA_NOPE:], mla_q_rope_norm), cos, sin)
    k_nope = rms_norm(kv[..., :MLA_NOPE], mla_k_nope_norm)
    mv = kv[..., MLA_NOPE:]
    k_rope = apply_rope(rms_norm(mla_kr, mla_k_rope_norm).reshape(b, s, 1, MLA_ROPE), cos, sin)
    k_rope = jnp.broadcast_to(k_rope, (b, s, MLA_HEADS, MLA_ROPE))
    mq = jnp.concatenate([q_nope, q_rope], axis=-1)
    mk = jnp.concatenate([k_nope, k_rope], axis=-1)
    mla_out = causal_block_attention(mq, mk, mv)

    gq = gla_q.reshape(b, s, GLA_HEADS, GLA_DK)
    gk = gla_k.reshape(b, s, GLA_HEADS, GLA_DK)
    gv = gla_v.reshape(b, s, GLA_HEADS, GLA_DV)
    log_a = jax.nn.log_sigmoid((gla_a_low @ gla_w_gate_up + gla_gate_bias).astype(jnp.float32)) / GLA_TAU
    log_a = log_a.reshape(b, s, GLA_HEADS, GLA_DK)
    gy = gated_linear_attention(gq, gk, gv, log_a).reshape(b, s, GLA_HEADS, GLA_DV)
    gy = rms_norm(gy, gla_out_norm.reshape(GLA_HEADS, GLA_DV)).reshape(b, s, GLA_HEADS * GLA_DV)
    gla_out = jax.nn.silu(gla_r) * gy

    return jnp.concatenate([ret_out, mla_out, gla_out], axis=-1) @ w_out


def setup_inputs(seed: int = 0) -> dict:
    key = jax.random.key(seed)
    ks = iter(jax.random.split(key, 32))

    def dense(shape):
        return jax.random.normal(next(ks), shape, jnp.float32) * (shape[-2] ** -0.5)

    def gain(n):
        return 1.0 + 0.02 * jax.random.normal(next(ks), (DEPTH, n), jnp.float32)

    x = jax.random.normal(next(ks), (BATCH, SEQ, D_MODEL), jnp.float32)
    positions = jnp.broadcast_to(jnp.arange(SEQ, dtype=jnp.int32), (BATCH, SEQ))
    return {
        "x": x,
        "positions": positions,
        "ffn1_norm": gain(D_MODEL),
        "ffn1_w_gate_up": dense((DEPTH, D_MODEL, 2 * D_FF)),
        "ffn1_w_down": dense((DEPTH, D_FF, D_MODEL)),
        "mix_norm": gain(D_MODEL),
        "w_in": dense((DEPTH, D_MODEL, IN_COLS)),
        "ret_out_norm": gain(RET_HEADS * RET_DV),
        "mla_q_norm": gain(MLA_Q_RANK),
        "mla_w_uq": dense((DEPTH, MLA_Q_RANK, MLA_HEADS * (MLA_NOPE + MLA_ROPE))),
        "mla_kv_norm": gain(MLA_KV_RANK),
        "mla_w_ukv": dense((DEPTH, MLA_KV_RANK, MLA_HEADS * (MLA_NOPE + MLA_DV))),
        "mla_q_nope_norm": gain(MLA_NOPE),
        "mla_q_rope_norm": gain(MLA_ROPE),
        "mla_k_nope_norm": gain(MLA_NOPE),
        "mla_k_rope_norm": gain(MLA_ROPE),
        "gla_w_gate_up": dense((DEPTH, GLA_GATE_RANK, GLA_HEADS * GLA_DK)),
        "gla_gate_bias": 0.1 * jax.random.normal(next(ks), (DEPTH, GLA_HEADS * GLA_DK), jnp.float32),
        "gla_out_norm": gain(GLA_HEADS * GLA_DV),
        "w_out": dense((DEPTH, D_MIX, D_MODEL)),
        "ffn2_norm": gain(D_MODEL),
        "ffn2_w_gate_up": dense((DEPTH, D_MODEL, 2 * D_FF)),
        "ffn2_w_down": dense((DEPTH, D_FF, D_MODEL)),
    }


def reference(x, positions, ffn1_norm, ffn1_w_gate_up, ffn1_w_down, mix_norm, w_in, ret_out_norm,
              mla_q_norm, mla_w_uq, mla_kv_norm, mla_w_ukv,
              mla_q_nope_norm, mla_q_rope_norm, mla_k_nope_norm, mla_k_rope_norm,
              gla_w_gate_up, gla_gate_bias, gla_out_norm, w_out,
              ffn2_norm, ffn2_w_gate_up, ffn2_w_down):
    cos, sin = rope_tables(positions, HEAD_DIM)
    for l in range(DEPTH):
        x = x + 0.5 * swiglu(rms_norm(x, ffn1_norm[l]), ffn1_w_gate_up[l], ffn1_w_down[l])
        h = rms_norm(x, mix_norm[l])
        x = x + hybrid_mixer(h, cos, sin, w_in[l], ret_out_norm[l],
                             mla_q_norm[l], mla_w_uq[l], mla_kv_norm[l], mla_w_ukv[l],
                             mla_q_nope_norm[l], mla_q_rope_norm[l], mla_k_nope_norm[l], mla_k_rope_norm[l],
                             gla_w_gate_up[l], gla_gate_bias[l], gla_out_norm[l], w_out[l])
        x = x + 0.5 * swiglu(rms_norm(x, ffn2_norm[l]), ffn2_w_gate_up[l], ffn2_w_down[l])
    return x
```

```python
import functools
import math

import numpy as np
import jax
import jax.numpy as jnp
from jax import lax
from jax.experimental import pallas as pl
from jax.experimental.pallas import tpu as pltpu

F32 = jnp.float32
BF16 = jnp.bfloat16

EPS = 1e-6
ROPE_THETA = 10000.0
HEAD_DIM = 64
N_HEADS = 4
GROUP = N_HEADS * HEAD_DIM
MLA_NOPE = 128
MLA_ROPE = 64
MLA_DV = 128
MLA_Q_RANK = 384
MLA_KV_RANK = 256
MLA_SLOT = 256
GLA_GATE_RANK = 16
GLA_TAU = 16.0
CHUNK = 128
LEAF = 8
GLA_LEVELS = (64, 32, 16, 8)
NEG = -1e30

LANE = 128
MXU_N = 256
VMEM_LIMIT_BYTES = 56 * 1024 * 1024

NT_DIMS = (((1,), (1,)), ((), ()))
TN_DIMS = (((0,), (0,)), ((), ()))


def _params(*sem):
    return pltpu.CompilerParams(dimension_semantics=sem, vmem_limit_bytes=VMEM_LIMIT_BYTES)


def _const_spec(shape):
    nd = len(shape)
    return pl.BlockSpec(shape, lambda *_: (0,) * nd)


def _dot(a, b):
    return jnp.dot(a, b, preferred_element_type=F32)


def _rms(x, g):
    return x * lax.rsqrt(jnp.mean(x * x, axis=-1, keepdims=True) + EPS) * g


def _silu(x):
    return x * jax.nn.sigmoid(x)


def _rope_table_body(pos_ref, inv_ref, sign_ref, cos_ref, sin_ref):
    ang = pos_ref[...].astype(F32) * inv_ref[...]
    cos_ref[...] = jnp.cos(ang)
    sin_ref[...] = jnp.sin(ang) * sign_ref[...]


def _rope_tables(positions, tm):
    t = positions.size
    half = HEAD_DIM // 2
    inv = ROPE_THETA ** (-jnp.arange(0, HEAD_DIM, 2, dtype=F32) / HEAD_DIM)
    inv_row = jnp.tile(inv, LANE // half).reshape(1, LANE)
    sign_row = jnp.tile(jnp.concatenate([-jnp.ones(half, F32), jnp.ones(half, F32)]), LANE // HEAD_DIM).reshape(1, LANE)
    pos = positions.reshape(t, 1)
    row = pl.BlockSpec((tm, LANE), lambda i: (i, 0))
    return pl.pallas_call(
        _rope_table_body,
        grid=(t // tm,),
        in_specs=[pl.BlockSpec((tm, 1), lambda i: (i, 0)), _const_spec((1, LANE)), _const_spec((1, LANE))],
        out_specs=[row, row],
        out_shape=[jax.ShapeDtypeStruct((t, LANE), F32)] * 2,
        compiler_params=_params("parallel"),
        name="rope_tables",
    )(pos, inv_row, sign_row)


def _rope(x, cos, sin):
    w = x.shape[-1]
    reps = w // LANE
    if reps > 1:
        cos = jnp.concatenate([cos] * reps, axis=-1)
        sin = jnp.concatenate([sin] * reps, axis=-1)
    lane = lax.broadcasted_iota(jnp.int32, x.shape, 1)
    first_half = (lane & (HEAD_DIM // 2)) == 0
    swapped = jnp.where(first_half, pltpu.roll(x, w - HEAD_DIM // 2, 1), pltpu.roll(x, HEAD_DIM // 2, 1))
    return x * cos + swapped * sin


def _ffn_body(x_ref, g_ref, wgu_ref, wd_ref, o_ref, act_ref, *, d_ff, d_model):
    x = x_ref[...]
    xn = _rms(x, g_ref[...]).astype(BF16)
    for c in range(d_ff // MXU_N):
        cols = slice(c * MXU_N, (c + 1) * MXU_N)
        ucols = slice(d_ff + c * MXU_N, d_ff + (c + 1) * MXU_N)
        g = _dot(xn, wgu_ref[:, cols])
        u = _dot(xn, wgu_ref[:, ucols])
        act_ref[:, cols] = (_silu(g) * u).astype(BF16)
    act = act_ref[...]
    for c in range(d_model // MXU_N):
        cols = slice(c * MXU_N, (c + 1) * MXU_N)
        o_ref[:, cols] = x_ref[:, cols] + 0.5 * _dot(act, wd_ref[:, cols])


def _ffn(x, norm, w_gate_up, w_down, tm):
    t, d_model = x.shape
    d_ff = w_down.shape[0]
    row = pl.BlockSpec((tm, d_model), lambda i: (i, 0))
    return pl.pallas_call(
        functools.partial(_ffn_body, d_ff=d_ff, d_model=d_model),
        grid=(t // tm,),
        in_specs=[row, _const_spec((1, d_model)), _const_spec((d_model, 2 * d_ff)), _const_spec((d_ff, d_model))],
        out_specs=row,
        out_shape=jax.ShapeDtypeStruct((t, d_model), F32),
        scratch_shapes=[pltpu.VMEM((tm, d_ff), BF16)],
        compiler_params=_params("parallel"),
        name="ffn",
    )(x, norm.reshape(1, d_model), w_gate_up.astype(BF16), w_down.astype(BF16))


W_RET = 0
W_CQ = 4 * GROUP
W_CKV = W_CQ + MLA_Q_RANK
W_GLA = W_CKV + MLA_KV_RANK
W_GLA_R = W_GLA + 3 * GROUP
W_TAIL = W_GLA_R + GROUP
W_COLS = W_TAIL + LANE


def _inproj_body(x_ref, g_ref, cos_ref, sin_ref, w_ref, qn_ref, wuq_ref, kvn_ref, wukv_ref,
                 gqn_ref, gqr_ref, gkn_ref, gkr_ref, wga_ref, ba_ref,
                 rq_ref, rk_ref, rv_ref, rg_ref, mq_ref, mk_ref, mv_ref,
                 gq_ref, gk_ref, gv_ref, la_ref, gg_ref):
    cos = cos_ref[...]
    sin = sin_ref[...]
    xn = _rms(x_ref[...], g_ref[...]).astype(BF16)
    tm = xn.shape[0]
    lane = lax.broadcasted_iota(jnp.int32, (tm, LANE), 1)
    low_half = lane < HEAD_DIM
    dk_scale = HEAD_DIM ** -0.5

    def proj(lo, width):
        return _dot(xn, w_ref[:, lo:lo + width])

    rq_ref[...] = _rope(proj(W_RET, GROUP), cos, sin).astype(BF16)
    rk_ref[...] = (_rope(proj(W_RET + GROUP, GROUP), cos, sin) * dk_scale).astype(BF16)
    rv_ref[...] = proj(W_RET + 2 * GROUP, GROUP).astype(BF16)
    rg_ref[...] = _silu(proj(W_RET + 3 * GROUP, GROUP)).astype(BF16)

    gq_ref[...] = proj(W_GLA, GROUP) * dk_scale
    gk_ref[...] = proj(W_GLA + GROUP, GROUP)
    gv_ref[...] = proj(W_GLA + 2 * GROUP, GROUP).astype(BF16)
    gg_ref[...] = _silu(proj(W_GLA_R, GROUP)).astype(BF16)
    tail = proj(W_TAIL, LANE)
    z = _dot(tail.astype(BF16), wga_ref[...]) + ba_ref[...]
    la_ref[...] = (jnp.minimum(z, 0.0) - jnp.log(1.0 + jnp.exp(-jnp.abs(z)))) * (1.0 / GLA_TAU)

    kr_ms = jnp.sum(jnp.where(low_half, tail * tail, 0.0), axis=-1, keepdims=True) * (1.0 / MLA_ROPE)
    kr = _rope(tail * lax.rsqrt(kr_ms + EPS) * gkr_ref[...], cos, sin).astype(BF16)
    ckv = _rms(proj(W_CKV, MLA_KV_RANK), kvn_ref[...]).astype(BF16)
    ones_col = jnp.where(lane == 0, 1.0, 0.0).astype(BF16)
    for h in range(N_HEADS):
        kn = _dot(ckv, wukv_ref[:, h * MLA_NOPE:(h + 1) * MLA_NOPE])
        mk_ref[:, h * MLA_SLOT:h * MLA_SLOT + MLA_NOPE] = _rms(kn, gkn_ref[...]).astype(BF16)
        mk_ref[:, h * MLA_SLOT + MLA_NOPE:(h + 1) * MLA_SLOT] = kr
        vh = _dot(ckv, wukv_ref[:, N_HEADS * MLA_NOPE + h * MLA_DV:N_HEADS * MLA_NOPE + (h + 1) * MLA_DV])
        mv_ref[:, h * MLA_SLOT:h * MLA_SLOT + MLA_DV] = vh.astype(BF16)
        mv_ref[:, h * MLA_SLOT + MLA_DV:(h + 1) * MLA_SLOT] = ones_col

    sm_scale = (MLA_NOPE + MLA_ROPE) ** -0.5
    cq = _rms(proj(W_CQ, MLA_Q_RANK), qn_ref[...]).astype(BF16)
    for h in range(N_HEADS):
        qn = _dot(cq, wuq_ref[:, h * MLA_NOPE:(h + 1) * MLA_NOPE])
        mq_ref[:, h * MLA_SLOT:h * MLA_SLOT + MLA_NOPE] = (_rms(qn, gqn_ref[...]) * sm_scale).astype(BF16)
    for pair in range(N_HEADS // 2):
        lo = N_HEADS * MLA_NOPE + pair * LANE
        qr = _dot(cq, wuq_ref[:, lo:lo + LANE])
        q2 = qr * qr
        ms_lo = jnp.sum(jnp.where(low_half, q2, 0.0), axis=-1, keepdims=True)
        ms_hi = jnp.sum(jnp.where(low_half, 0.0, q2), axis=-1, keepdims=True)
        ms = jnp.where(low_half, ms_lo, ms_hi) * (1.0 / MLA_ROPE)
        qr = _rope(qr * lax.rsqrt(ms + EPS) * gqr_ref[...], cos, sin) * sm_scale
        h0 = 2 * pair
        mq_ref[:, h0 * MLA_SLOT + MLA_NOPE:(h0 + 1) * MLA_SLOT] = jnp.where(low_half, qr, 0.0).astype(BF16)
        mq_ref[:, (h0 + 1) * MLA_SLOT + MLA_NOPE:(h0 + 2) * MLA_SLOT] = jnp.where(
            low_half, pltpu.roll(qr, HEAD_DIM, 1), 0.0).astype(BF16)


def _reorder_w_in(w_in):
    d = w_in.shape[0]
    o = np.cumsum([0, GROUP, GROUP, GROUP, GROUP, MLA_Q_RANK, MLA_KV_RANK, MLA_ROPE,
                   GROUP, GROUP, GROUP, GLA_GATE_RANK, GROUP])
    seg = [w_in[:, o[i]:o[i + 1]] for i in range(12)]
    ret_q, ret_k, ret_v, ret_g, cq, ckv, kr, gq, gk, gv, a_low, gr = seg
    pad = jnp.zeros((d, LANE - MLA_ROPE - GLA_GATE_RANK), w_in.dtype)
    return jnp.concatenate([ret_q, ret_k, ret_v, ret_g, cq, ckv, gq, gk, gv, gr, kr, a_low, pad], axis=1)


def _inproj(x, cos, sin, norm, w_in, mla_q_norm, mla_w_uq, mla_kv_norm, mla_w_ukv,
            q_nope_norm, q_rope_norm, k_nope_norm, k_rope_norm, gla_w_gate_up, gla_gate_bias, tm):
    t, d = x.shape
    w = _reorder_w_in(w_in).astype(BF16)
    qd = MLA_NOPE + MLA_ROPE
    wuq = mla_w_uq.reshape(MLA_Q_RANK, N_HEADS, qd)
    wuq = jnp.concatenate([wuq[:, :, :MLA_NOPE].reshape(MLA_Q_RANK, -1),
                           wuq[:, :, MLA_NOPE:].reshape(MLA_Q_RANK, -1)], axis=1).astype(BF16)
    wukv = mla_w_ukv.reshape(MLA_KV_RANK, N_HEADS, MLA_NOPE + MLA_DV)
    wukv = jnp.concatenate([wukv[:, :, :MLA_NOPE].reshape(MLA_KV_RANK, -1),
                            wukv[:, :, MLA_NOPE:].reshape(MLA_KV_RANK, -1)], axis=1).astype(BF16)
    gqr = jnp.tile(q_rope_norm, LANE // MLA_ROPE).reshape(1, LANE)
    gkr = jnp.concatenate([k_rope_norm, jnp.zeros(LANE - MLA_ROPE, F32)]).reshape(1, LANE)
    wga = jnp.zeros((LANE, GROUP), F32).at[MLA_ROPE:MLA_ROPE + GLA_GATE_RANK].set(gla_w_gate_up).astype(BF16)

    row = lambda n: pl.BlockSpec((tm, n), lambda i: (i, 0))
    out = lambda n, dt: jax.ShapeDtypeStruct((t, n), dt)
    wide = N_HEADS * MLA_SLOT
    return pl.pallas_call(
        _inproj_body,
        grid=(t // tm,),
        in_specs=[row(d), _const_spec((1, d)), row(LANE), row(LANE), _const_spec((d, W_COLS)),
                  _const_spec((1, MLA_Q_RANK)), _const_spec(wuq.shape), _const_spec((1, MLA_KV_RANK)),
                  _const_spec(wukv.shape), _const_spec((1, MLA_NOPE)), _const_spec((1, LANE)),
                  _const_spec((1, MLA_NOPE)), _const_spec((1, LANE)), _const_spec((LANE, GROUP)),
                  _const_spec((1, GROUP))],
        out_specs=[row(GROUP)] * 4 + [row(wide)] * 3 + [row(GROUP)] * 5,
        out_shape=[out(GROUP, BF16)] * 4 + [out(wide, BF16)] * 3
                  + [out(GROUP, F32), out(GROUP, F32), out(GROUP, BF16), out(GROUP, F32), out(GROUP, BF16)],
        compiler_params=_params("parallel"),
        name="inproj",
    )(x, norm.reshape(1, d), cos, sin, w, mla_q_norm.reshape(1, -1), wuq, mla_kv_norm.reshape(1, -1), wukv,
      q_nope_norm.reshape(1, -1), gqr, k_nope_norm.reshape(1, -1), gkr, wga, gla_gate_bias.reshape(1, -1))


def _head_masks(shape):
    lane = lax.broadcasted_iota(jnp.int32, shape, 1)
    return [(lane >> int(math.log2(HEAD_DIM))) == h for h in range(N_HEADS)]


def _stack_heads(x, masks):
    return jnp.concatenate([jnp.where(m, x, jnp.zeros_like(x)) for m in masks], axis=0)


def _unstack_heads(y, masks, c):
    out = jnp.where(masks[0], y[0:c], 0.0)
    for h in range(1, N_HEADS):
        out = out + jnp.where(masks[h], y[h * c:(h + 1) * c], 0.0)
    return out


def _head_norm_gate(o, gate, gmat, norm):
    ms = _dot((o * o).astype(BF16), gmat) * (1.0 / HEAD_DIM)
    return (gate.astype(F32) * (o * lax.rsqrt(ms + EPS) * norm)).astype(BF16)


def _block_diag(dtype):
    idx = np.arange(GROUP) // HEAD_DIM
    return jnp.asarray((idx[:, None] == idx[None, :]).astype(np.float32), dtype)


def _ret_body(q_ref, k_ref, v_ref, g_ref, intra_ref, xi_ref, zeta_ref, cd_ref, gmat_ref, norm_ref,
              o_ref, state_ref, *, n_chunks):
    @pl.when(pl.program_id(1) == 0)
    def _():
        state_ref[...] = jnp.zeros_like(state_ref)

    masks = _head_masks((CHUNK, GROUP))
    for c in range(n_chunks):
        rows = slice(c * CHUNK, (c + 1) * CHUNK)
        q = q_ref[rows, :]
        k = k_ref[rows, :]
        v = v_ref[rows, :]
        sc = lax.dot_general(_stack_heads(q, masks), k, NT_DIMS, preferred_element_type=F32) * intra_ref[...]
        o = _unstack_heads(_dot(sc.astype(BF16), v), masks, CHUNK)
        state = state_ref[...]
        o = o + _dot(q, state.astype(BF16)) * xi_ref[...]
        vz = (v.astype(F32) * zeta_ref[...]).astype(BF16)
        kv = lax.dot_general(k, vz, TN_DIMS, preferred_element_type=F32)
        state_ref[...] = state * cd_ref[...] + kv * (cd_ref[...] > 0.0).astype(F32)
        o_ref[rows, :] = _head_norm_gate(o, g_ref[rows, :], gmat_ref[...], norm_ref[...])


def _retention(q, k, v, gate, norm, batch, rows_per_step):
    t = q.shape[0]
    seq = t // batch
    steps = seq // rows_per_step
    log_g = np.log1p(-np.power(2.0, -5.0 - np.arange(N_HEADS, dtype=np.float64)))
    idx = np.arange(CHUNK, dtype=np.float64)
    diff = idx[:, None] - idx[None, :]
    intra = np.where(diff >= 0, np.exp(log_g[:, None, None] * np.maximum(diff, 0.0)), 0.0)
    intra = intra.reshape(N_HEADS * CHUNK, CHUNK)
    lane_g = np.repeat(log_g, HEAD_DIM)[None, :]
    xi = np.exp(lane_g * (idx[:, None] + 1.0))
    zeta = np.exp(lane_g * (CHUNK - 1.0 - idx[:, None]))
    head = np.arange(GROUP) // HEAD_DIM
    cd = (head[:, None] == head[None, :]) * np.exp(lane_g * CHUNK)
    consts = [jnp.asarray(a, F32) for a in (intra, xi, zeta, cd)]
    row = pl.BlockSpec((rows_per_step, GROUP), lambda b, i: (b * steps + i, 0))
    return pl.pallas_call(
        functools.partial(_ret_body, n_chunks=rows_per_step // CHUNK),
        grid=(batch, steps),
        in_specs=[row] * 4 + [_const_spec(c.shape) for c in consts] + [_const_spec((GROUP, GROUP)), _const_spec((1, GROUP))],
        out_specs=row,
        out_shape=jax.ShapeDtypeStruct((t, GROUP), BF16),
        scratch_shapes=[pltpu.VMEM((GROUP, GROUP), F32)],
        compiler_params=_params("parallel", "arbitrary"),
        name="retention",
    )(q, k, v, gate, *consts, _block_diag(BF16), norm.reshape(1, GROUP))


def _gla_body(q_ref, k_ref, v_ref, a_ref, g_ref, ltri_ref, lmask_ref, gmat_ref, bd_ref, norm_ref,
              o_ref, state_ref, *, n_chunks):
    @pl.when(pl.program_id(1) == 0)
    def _():
        state_ref[...] = jnp.zeros_like(state_ref)

    masks = _head_masks((CHUNK, GROUP))
    row = lax.broadcasted_iota(jnp.int32, (CHUNK, GROUP), 0)
    gmat = gmat_ref[...]
    ltri = ltri_ref[...]
    for c in range(n_chunks):
        rows = slice(c * CHUNK, (c + 1) * CHUNK)
        q = q_ref[rows, :]
        k = k_ref[rows, :]
        vb = v_ref[rows, :]
        v = vb.astype(F32)
        a = a_ref[rows, :]
        a0 = a.astype(BF16)
        r1 = a - a0.astype(F32)
        a1 = r1.astype(BF16)
        a2 = (r1 - a1.astype(F32)).astype(BF16)
        b = _dot(ltri, a0) + _dot(ltri, a1) + _dot(ltri, a2)
        b_last = b[CHUNK - 1:CHUNK, :]

        state = state_ref[...]
        o = lax.dot_general((q * jnp.exp(b)).astype(BF16), state.astype(BF16), NT_DIMS, preferred_element_type=F32)
        kd = (k * jnp.exp(b_last - b)).astype(BF16)
        kv = lax.dot_general(vb, kd, TN_DIMS, preferred_element_type=F32)
        state_ref[...] = state * jnp.exp(b_last) + kv * bd_ref[...]

        sc = None
        for li, m in enumerate(GLA_LEVELS):
            bref = jnp.concatenate(
                [jnp.broadcast_to(b[(2 * j + 1) * m - 1:(2 * j + 1) * m, :], (2 * m, GROUP)) for j in range(CHUNK // (2 * m))],
                axis=0)
            upper = ((row >> int(math.log2(m))) & 1) == 1
            w = jnp.exp(jnp.where(upper, b - bref, bref - b))
            qt = jnp.where(upper, q * w, 0.0).astype(BF16)
            kt = jnp.where(upper, 0.0, k * w).astype(BF16)
            term = lax.dot_general(_stack_heads(qt, masks), kt, NT_DIMS, preferred_element_type=F32) * lmask_ref[li]
            sc = term if sc is None else sc + term
        o = o + _unstack_heads(_dot(sc.astype(BF16), vb), masks, CHUNK)

        prods = [(q * k).astype(BF16)]
        for d in range(1, LEAF):
            valid = (row & (LEAF - 1)) >= d
            w = jnp.exp(jnp.where(valid, b - pltpu.roll(b, d, 0), NEG))
            prods.append((q * pltpu.roll(k, d, 0) * w).astype(BF16))
        sums = _dot(jnp.concatenate(prods, axis=0), gmat)
        o = o + sums[0:CHUNK] * v
        for d in range(1, LEAF):
            o = o + sums[d * CHUNK:(d + 1) * CHUNK] * pltpu.roll(v, d, 0)

        o_ref[rows, :] = _head_norm_gate(o, g_ref[rows, :], gmat, norm_ref[...])


def _gla(q, k, v, log_a, gate, norm, batch, rows_per_step):
    t = q.shape[0]
    seq = t // batch
    steps = seq // rows_per_step
    idx = np.arange(CHUNK)
    ltri = jnp.asarray(idx[:, None] >= idx[None, :], BF16)
    lm = []
    for m in GLA_LEVELS:
        same = (idx[:, None] // (2 * m)) == (idx[None, :] // (2 * m))
        ok = same & (((idx[:, None] // m) & 1) == 1) & (((idx[None, :] // m) & 1) == 0)
        lm.append(np.tile(ok, (N_HEADS, 1)))
    lmask = jnp.asarray(np.stack(lm), F32)
    row = pl.BlockSpec((rows_per_step, GROUP), lambda b, i: (b * steps + i, 0))
    return pl.pallas_call(
        functools.partial(_gla_body, n_chunks=rows_per_step // CHUNK),
        grid=(batch, steps),
        in_specs=[row] * 5 + [_const_spec((CHUNK, CHUNK)), _const_spec(lmask.shape), _const_spec((GROUP, GROUP)),
                              _const_spec((GROUP, GROUP)), _const_spec((1, GROUP))],
        out_specs=row,
        out_shape=jax.ShapeDtypeStruct((t, GROUP), BF16),
        scratch_shapes=[pltpu.VMEM((GROUP, GROUP), F32)],
        compiler_params=_params("parallel", "arbitrary"),
        name="gla",
    )(q, k, v, log_a, gate, ltri, lmask, _block_diag(BF16), _block_diag(F32), norm.reshape(1, GROUP))


def _mla_body(q_ref, k_ref, v_ref, o_ref, m_ref, acc_ref, *, tq):
    i = pl.program_id(2)
    q = q_ref[...]
    m_ref[...] = jnp.full(m_ref.shape, NEG, F32)
    acc_ref[...] = jnp.zeros_like(acc_ref)

    def step(j, diagonal):
        start = pl.multiple_of(j * tq, tq)
        k = k_ref[pl.ds(start, tq), :]
        v = v_ref[pl.ds(start, tq), :]
        s = lax.dot_general(q, k, NT_DIMS, preferred_element_type=F32)
        if diagonal:
            r = lax.broadcasted_iota(jnp.int32, s.shape, 0)
            cidx = lax.broadcasted_iota(jnp.int32, s.shape, 1)
            s = jnp.where(cidx <= r, s, NEG)
        m_prev = m_ref[...]
        m_new = jnp.maximum(m_prev, jnp.max(s, axis=-1, keepdims=True))
        p = jnp.exp(s - m_new).astype(BF16)
        acc_ref[...] = acc_ref[...] * jnp.exp(m_prev - m_new) + _dot(p, v)
        m_ref[...] = m_new

    def body(j, carry):
        step(j, False)
        return carry

    lax.fori_loop(0, i, body, 0)
    step(i, True)
    acc = acc_ref[...]
    o_ref[...] = (acc[:, :MLA_DV] / acc[:, MLA_DV:MLA_DV + 1]).astype(BF16)


def _mla_attention(q, k, v, batch, tq):
    t = q.shape[0]
    seq = t // batch
    nq = seq // tq
    q3, k3, v3 = (a.reshape(batch, seq, N_HEADS * MLA_SLOT) for a in (q, k, v))
    kv_spec = pl.BlockSpec((None, seq, MLA_SLOT), lambda b, h, i: (b, 0, h))
    out = pl.pallas_call(
        functools.partial(_mla_body, tq=tq),
        grid=(batch, N_HEADS, nq),
        in_specs=[pl.BlockSpec((None, tq, MLA_SLOT), lambda b, h, i: (b, i, h)), kv_spec, kv_spec],
        out_specs=pl.BlockSpec((None, tq, MLA_DV), lambda b, h, i: (b, i, h)),
        out_shape=jax.ShapeDtypeStruct((batch, seq, N_HEADS * MLA_DV), BF16),
        scratch_shapes=[pltpu.VMEM((tq, 1), F32), pltpu.VMEM((tq, MLA_SLOT), F32)],
        compiler_params=_params("parallel", "parallel", "arbitrary"),
        name="mla_attention",
    )(q3, k3, v3)
    return out.reshape(t, N_HEADS * MLA_DV)


def _outproj_body(x_ref, r_ref, m_ref, g_ref, wr_ref, wm_ref, wg_ref, o_ref, *, d_model):
    r = r_ref[...]
    m = m_ref[...]
    g = g_ref[...]
    for c in range(d_model // MXU_N):
        cols = slice(c * MXU_N, (c + 1) * MXU_N)
        o_ref[:, cols] = x_ref[:, cols] + (_dot(r, wr_ref[:, cols]) + _dot(m, wm_ref[:, cols]) + _dot(g, wg_ref[:, cols]))


def _outproj(x, ret, mla, gla, w_out, tm):
    t, d = x.shape
    nr, nm, ng = ret.shape[1], mla.shape[1], gla.shape[1]
    w = w_out.astype(BF16)
    row = lambda n: pl.BlockSpec((tm, n), lambda i: (i, 0))
    return pl.pallas_call(
        functools.partial(_outproj_body, d_model=d),
        grid=(t // tm,),
        in_specs=[row(d), row(nr), row(nm), row(ng), _const_spec((nr, d)), _const_spec((nm, d)), _const_spec((ng, d))],
        out_specs=row(d),
        out_shape=jax.ShapeDtypeStruct((t, d), F32),
        compiler_params=_params("parallel"),
        name="outproj",
    )(x, ret, mla, gla, w[:nr], w[nr:nr + nm], w[nr + nm:])


def _tiles(batch, seq):
    tm = min(512, seq)
    rows_per_step = min(1024, seq)
    tq = min(1024, seq)
    return tm, rows_per_step, tq


def kernel(x, positions, ffn1_norm, ffn1_w_gate_up, ffn1_w_down, mix_norm, w_in, ret_out_norm, mla_q_norm, mla_w_uq, mla_kv_norm, mla_w_ukv, mla_q_nope_norm, mla_q_rope_norm, mla_k_nope_norm, mla_k_rope_norm, gla_w_gate_up, gla_gate_bias, gla_out_norm, w_out, ffn2_norm, ffn2_w_gate_up, ffn2_w_down):
    batch, seq, d_model = x.shape
    depth = w_in.shape[0]
    tm, rows_per_step, tq = _tiles(batch, seq)
    xt = x.reshape(batch * seq, d_model)
    cos, sin = _rope_tables(positions, tm)
    for l in range(depth):
        xt = _ffn(xt, ffn1_norm[l], ffn1_w_gate_up[l], ffn1_w_down[l], tm)
        (rq, rk, rv, rg, mq, mk, mv, gq, gk, gv, la, gg) = _inproj(
            xt, cos, sin, mix_norm[l], w_in[l], mla_q_norm[l], mla_w_uq[l], mla_kv_norm[l], mla_w_ukv[l],
            mla_q_nope_norm[l], mla_q_rope_norm[l], mla_k_nope_norm[l], mla_k_rope_norm[l],
            gla_w_gate_up[l], gla_gate_bias[l], tm)
        ret = _retention(rq, rk, rv, rg, ret_out_norm[l], batch, rows_per_step)
        mla = _mla_attention(mq, mk, mv, batch, tq)
        gla = _gla(gq, gk, gv, la, gg, gla_out_norm[l], batch, rows_per_step)
        xt = _outproj(xt, ret, mla, gla, w_out[l], tm)
        xt = _ffn(xt, ffn2_norm[l], ffn2_w_gate_up[l], ffn2_w_down[l], tm)
    return xt.reshape(batch, seq, d_model)
```

```python
import functools
import math

import numpy as np
import jax
import jax.numpy as jnp
from jax import lax
from jax.experimental import pallas as pl
from jax.experimental.pallas import tpu as pltpu

F32 = jnp.float32
BF16 = jnp.bfloat16

EPS = 1e-6
ROPE_THETA = 10000.0
HEAD_DIM = 64
N_HEADS = 4
GROUP = N_HEADS * HEAD_DIM
MLA_NOPE = 128
MLA_ROPE = 64
MLA_DV = 128
MLA_Q_RANK = 384
MLA_KV_RANK = 256
MLA_SLOT = 256
GLA_GATE_RANK = 16
GLA_TAU = 16.0
CHUNK = 128
LEAF = 8
GLA_LEVELS = (64, 32, 16, 8)
NEG = -1e30

LANE = 128
MXU_N = 256
VMEM_LIMIT_BYTES = 56 * 1024 * 1024

NT_DIMS = (((1,), (1,)), ((), ()))
TN_DIMS = (((0,), (0,)), ((), ()))


def _params(*sem):
    return pltpu.CompilerParams(dimension_semantics=sem, vmem_limit_bytes=VMEM_LIMIT_BYTES)


def _const_spec(shape):
    nd = len(shape)
    return pl.BlockSpec(shape, lambda *_: (0,) * nd)


def _dot(a, b):
    return jnp.dot(a, b, preferred_element_type=F32)


def _rms(x, g):
    return x * lax.rsqrt(jnp.mean(x * x, axis=-1, keepdims=True) + EPS) * g


def _silu(x):
    return x * jax.nn.sigmoid(x)


def _rope_table_body(pos_ref, inv_ref, sign_ref, cos_ref, sin_ref):
    ang = pos_ref[...].astype(F32) * inv_ref[...]
    cos_ref[...] = jnp.cos(ang)
    sin_ref[...] = jnp.sin(ang) * sign_ref[...]


def _rope_tables(positions, tm):
    t = positions.size
    half = HEAD_DIM // 2
    inv = ROPE_THETA ** (-jnp.arange(0, HEAD_DIM, 2, dtype=F32) / HEAD_DIM)
    inv_row = jnp.tile(inv, LANE // half).reshape(1, LANE)
    sign_row = jnp.tile(jnp.concatenate([-jnp.ones(half, F32), jnp.ones(half, F32)]), LANE // HEAD_DIM).reshape(1, LANE)
    pos = positions.reshape(t, 1)
    row = pl.BlockSpec((tm, LANE), lambda i: (i, 0))
    return pl.pallas_call(
        _rope_table_body,
        grid=(t // tm,),
        in_specs=[pl.BlockSpec((tm, 1), lambda i: (i, 0)), _const_spec((1, LANE)), _const_spec((1, LANE))],
        out_specs=[row, row],
        out_shape=[jax.ShapeDtypeStruct((t, LANE), F32)] * 2,
        compiler_params=_params("parallel"),
        name="rope_tables",
    )(pos, inv_row, sign_row)


def _rope(x, cos, sin):
    w = x.shape[-1]
    reps = w // LANE
    if reps > 1:
        cos = jnp.concatenate([cos] * reps, axis=-1)
        sin = jnp.concatenate([sin] * reps, axis=-1)
    lane = lax.broadcasted_iota(jnp.int32, x.shape, 1)
    first_half = (lane & (HEAD_DIM // 2)) == 0
    swapped = jnp.where(first_half, pltpu.roll(x, w - HEAD_DIM // 2, 1), pltpu.roll(x, HEAD_DIM // 2, 1))
    return x * cos + swapped * sin


def _ffn_body(x_ref, g_ref, wgu_ref, wd_ref, o_ref, act_ref, *, d_ff, d_model):
    x = x_ref[...]
    xn = _rms(x, g_ref[...]).astype(BF16)
    for c in range(d_ff // MXU_N):
        cols = slice(c * MXU_N, (c + 1) * MXU_N)
        ucols = slice(d_ff + c * MXU_N, d_ff + (c + 1) * MXU_N)
        g = _dot(xn, wgu_ref[:, cols])
        u = _dot(xn, wgu_ref[:, ucols])
        act_ref[:, cols] = (_silu(g) * u).astype(BF16)
    act = act_ref[...]
    for c in range(d_model // MXU_N):
        cols = slice(c * MXU_N, (c + 1) * MXU_N)
        o_ref[:, cols] = x_ref[:, cols] + 0.5 * _dot(act, wd_ref[:, cols])


def _ffn(x, norm, w_gate_up, w_down, tm):
    t, d_model = x.shape
    d_ff = w_down.shape[0]
    row = pl.BlockSpec((tm, d_model), lambda i: (i, 0))
    return pl.pallas_call(
        functools.partial(_ffn_body, d_ff=d_ff, d_model=d_model),
        grid=(t // tm,),
        in_specs=[row, _const_spec((1, d_model)), _const_spec((d_model, 2 * d_ff)), _const_spec((d_ff, d_model))],
        out_specs=row,
        out_shape=jax.ShapeDtypeStruct((t, d_model), F32),
        scratch_shapes=[pltpu.VMEM((tm, d_ff), BF16)],
        compiler_params=_params("parallel"),
        name="ffn",
    )(x, norm.reshape(1, d_model), w_gate_up.astype(BF16), w_down.astype(BF16))


W_RET = 0
W_CQ = 4 * GROUP
W_CKV = W_CQ + MLA_Q_RANK
W_GLA = W_CKV + MLA_KV_RANK
W_GLA_R = W_GLA + 3 * GROUP
W_TAIL = W_GLA_R + GROUP
W_COLS = W_TAIL + LANE


def _inproj_body(x_ref, g_ref, cos_ref, sin_ref, w_ref, qn_ref, wuq_ref, kvn_ref, wukv_ref,
                 gqn_ref, gqr_ref, gkn_ref, gkr_ref, wga_ref, ba_ref,
                 rq_ref, rk_ref, rv_ref, rg_ref, mq_ref, mk_ref, mv_ref,
                 gq_ref, gk_ref, gv_ref, la_ref, gg_ref):
    cos = cos_ref[...]
    sin = sin_ref[...]
    xn = _rms(x_ref[...], g_ref[...]).astype(BF16)
    tm = xn.shape[0]
    lane = lax.broadcasted_iota(jnp.int32, (tm, LANE), 1)
    low_half = lane < HEAD_DIM
    dk_scale = HEAD_DIM ** -0.5

    def proj(lo, width):
        return _dot(xn, w_ref[:, lo:lo + width])

    rq_ref[...] = _rope(proj(W_RET, GROUP), cos, sin).astype(BF16)
    rk_ref[...] = (_rope(proj(W_RET + GROUP, GROUP), cos, sin) * dk_scale).astype(BF16)
    rv_ref[...] = proj(W_RET + 2 * GROUP, GROUP).astype(BF16)
    rg_ref[...] = _silu(proj(W_RET + 3 * GROUP, GROUP)).astype(BF16)

    gq_ref[...] = proj(W_GLA, GROUP) * dk_scale
    gk_ref[...] = proj(W_GLA + GROUP, GROUP)
    gv_ref[...] = proj(W_GLA + 2 * GROUP, GROUP).astype(BF16)
    gg_ref[...] = _silu(proj(W_GLA_R, GROUP)).astype(BF16)
    tail = proj(W_TAIL, LANE)
    z = _dot(tail.astype(BF16), wga_ref[...]) + ba_ref[...]
    la_ref[...] = (jnp.minimum(z, 0.0) - jnp.log(1.0 + jnp.exp(-jnp.abs(z)))) * (1.0 / GLA_TAU)

    kr_ms = jnp.sum(jnp.where(low_half, tail * tail, 0.0), axis=-1, keepdims=True) * (1.0 / MLA_ROPE)
    kr = _rope(tail * lax.rsqrt(kr_ms + EPS) * gkr_ref[...], cos, sin).astype(BF16)
    ckv = _rms(proj(W_CKV, MLA_KV_RANK), kvn_ref[...]).astype(BF16)
    ones_col = jnp.where(lane == 0, 1.0, 0.0).astype(BF16)
    for h in range(N_HEADS):
        kn = _dot(ckv, wukv_ref[:, h * MLA_NOPE:(h + 1) * MLA_NOPE])
        mk_ref[:, h * MLA_SLOT:h * MLA_SLOT + MLA_NOPE] = _rms(kn, gkn_ref[...]).astype(BF16)
        mk_ref[:, h * MLA_SLOT + MLA_NOPE:(h + 1) * MLA_SLOT] = kr
        vh = _dot(ckv, wukv_ref[:, N_HEADS * MLA_NOPE + h * MLA_DV:N_HEADS * MLA_NOPE + (h + 1) * MLA_DV])
        mv_ref[:, h * MLA_SLOT:h * MLA_SLOT + MLA_DV] = vh.astype(BF16)
        mv_ref[:, h * MLA_SLOT + MLA_DV:(h + 1) * MLA_SLOT] = ones_col

    sm_scale = (MLA_NOPE + MLA_ROPE) ** -0.5 * math.log2(math.e)
    cq = _rms(proj(W_CQ, MLA_Q_RANK), qn_ref[...]).astype(BF16)
    for h in range(N_HEADS):
        qn = _dot(cq, wuq_ref[:, h * MLA_NOPE:(h + 1) * MLA_NOPE])
        mq_ref[:, h * MLA_SLOT:h * MLA_SLOT + MLA_NOPE] = (_rms(qn, gqn_ref[...]) * sm_scale).astype(BF16)
    for pair in range(N_HEADS // 2):
        lo = N_HEADS * MLA_NOPE + pair * LANE
        qr = _dot(cq, wuq_ref[:, lo:lo + LANE])
        q2 = qr * qr
        ms_lo = jnp.sum(jnp.where(low_half, q2, 0.0), axis=-1, keepdims=True)
        ms_hi = jnp.sum(jnp.where(low_half, 0.0, q2), axis=-1, keepdims=True)
        ms = jnp.where(low_half, ms_lo, ms_hi) * (1.0 / MLA_ROPE)
        qr = _rope(qr * lax.rsqrt(ms + EPS) * gqr_ref[...], cos, sin) * sm_scale
        h0 = 2 * pair
        mq_ref[:, h0 * MLA_SLOT + MLA_NOPE:(h0 + 1) * MLA_SLOT] = jnp.where(low_half, qr, 0.0).astype(BF16)
        mq_ref[:, (h0 + 1) * MLA_SLOT + MLA_NOPE:(h0 + 2) * MLA_SLOT] = jnp.where(
            low_half, pltpu.roll(qr, HEAD_DIM, 1), 0.0).astype(BF16)


def _reorder_w_in(w_in):
    d = w_in.shape[0]
    o = np.cumsum([0, GROUP, GROUP, GROUP, GROUP, MLA_Q_RANK, MLA_KV_RANK, MLA_ROPE,
                   GROUP, GROUP, GROUP, GLA_GATE_RANK, GROUP])
    seg = [w_in[:, o[i]:o[i + 1]] for i in range(12)]
    ret_q, ret_k, ret_v, ret_g, cq, ckv, kr, gq, gk, gv, a_low, gr = seg
    pad = jnp.zeros((d, LANE - MLA_ROPE - GLA_GATE_RANK), w_in.dtype)
    return jnp.concatenate([ret_q, ret_k, ret_v, ret_g, cq, ckv, gq, gk, gv, gr, kr, a_low, pad], axis=1)


def _inproj(x, cos, sin, norm, w_in, mla_q_norm, mla_w_uq, mla_kv_norm, mla_w_ukv,
            q_nope_norm, q_rope_norm, k_nope_norm, k_rope_norm, gla_w_gate_up, gla_gate_bias, tm):
    t, d = x.shape
    w = _reorder_w_in(w_in).astype(BF16)
    qd = MLA_NOPE + MLA_ROPE
    wuq = mla_w_uq.reshape(MLA_Q_RANK, N_HEADS, qd)
    wuq = jnp.concatenate([wuq[:, :, :MLA_NOPE].reshape(MLA_Q_RANK, -1),
                           wuq[:, :, MLA_NOPE:].reshape(MLA_Q_RANK, -1)], axis=1).astype(BF16)
    wukv = mla_w_ukv.reshape(MLA_KV_RANK, N_HEADS, MLA_NOPE + MLA_DV)
    wukv = jnp.concatenate([wukv[:, :, :MLA_NOPE].reshape(MLA_KV_RANK, -1),
                            wukv[:, :, MLA_NOPE:].reshape(MLA_KV_RANK, -1)], axis=1).astype(BF16)
    gqr = jnp.tile(q_rope_norm, LANE // MLA_ROPE).reshape(1, LANE)
    gkr = jnp.concatenate([k_rope_norm, jnp.zeros(LANE - MLA_ROPE, F32)]).reshape(1, LANE)
    wga = jnp.zeros((LANE, GROUP), F32).at[MLA_ROPE:MLA_ROPE + GLA_GATE_RANK].set(gla_w_gate_up).astype(BF16)

    row = lambda n: pl.BlockSpec((tm, n), lambda i: (i, 0))
    out = lambda n, dt: jax.ShapeDtypeStruct((t, n), dt)
    wide = N_HEADS * MLA_SLOT
    return pl.pallas_call(
        _inproj_body,
        grid=(t // tm,),
        in_specs=[row(d), _const_spec((1, d)), row(LANE), row(LANE), _const_spec((d, W_COLS)),
                  _const_spec((1, MLA_Q_RANK)), _const_spec(wuq.shape), _const_spec((1, MLA_KV_RANK)),
                  _const_spec(wukv.shape), _const_spec((1, MLA_NOPE)), _const_spec((1, LANE)),
                  _const_spec((1, MLA_NOPE)), _const_spec((1, LANE)), _const_spec((LANE, GROUP)),
                  _const_spec((1, GROUP))],
        out_specs=[row(GROUP)] * 4 + [row(wide)] * 3 + [row(GROUP)] * 5,
        out_shape=[out(GROUP, BF16)] * 4 + [out(wide, BF16)] * 3
                  + [out(GROUP, F32), out(GROUP, F32), out(GROUP, BF16), out(GROUP, F32), out(GROUP, BF16)],
        compiler_params=_params("parallel"),
        name="inproj",
    )(x, norm.reshape(1, d), cos, sin, w, mla_q_norm.reshape(1, -1), wuq, mla_kv_norm.reshape(1, -1), wukv,
      q_nope_norm.reshape(1, -1), gqr, k_nope_norm.reshape(1, -1), gkr, wga, gla_gate_bias.reshape(1, -1))


def _head_masks(shape):
    lane = lax.broadcasted_iota(jnp.int32, shape, 1)
    return [(lane >> int(math.log2(HEAD_DIM))) == h for h in range(N_HEADS)]


def _stack_heads(x, masks):
    return jnp.concatenate([jnp.where(m, x, jnp.zeros_like(x)) for m in masks], axis=0)


def _unstack_heads(y, masks, c):
    out = jnp.where(masks[0], y[0:c], 0.0)
    for h in range(1, N_HEADS):
        out = out + jnp.where(masks[h], y[h * c:(h + 1) * c], 0.0)
    return out


def _head_norm_gate(o, gate, gmat, norm):
    ms = _dot((o * o).astype(BF16), gmat) * (1.0 / HEAD_DIM)
    return (gate.astype(F32) * (o * lax.rsqrt(ms + EPS) * norm)).astype(BF16)


def _block_diag(dtype):
    idx = np.arange(GROUP) // HEAD_DIM
    return jnp.asarray((idx[:, None] == idx[None, :]).astype(np.float32), dtype)


def _ret_body(q_ref, k_ref, v_ref, g_ref, intra_ref, xi_ref, zeta_ref, cd_ref, gmat_ref, norm_ref,
              o_ref, state_ref, *, n_chunks):
    @pl.when(pl.program_id(1) == 0)
    def _():
        state_ref[...] = jnp.zeros_like(state_ref)

    masks = _head_masks((CHUNK, GROUP))
    for c in range(n_chunks):
        rows = slice(c * CHUNK, (c + 1) * CHUNK)
        q = q_ref[rows, :]
        k = k_ref[rows, :]
        v = v_ref[rows, :]
        sc = lax.dot_general(_stack_heads(q, masks), k, NT_DIMS, preferred_element_type=F32) * intra_ref[...]
        o = _unstack_heads(_dot(sc.astype(BF16), v), masks, CHUNK)
        state = state_ref[...]
        o = o + _dot(q, state.astype(BF16)) * xi_ref[...]
        vz = (v.astype(F32) * zeta_ref[...]).astype(BF16)
        kv = lax.dot_general(k, vz, TN_DIMS, preferred_element_type=F32)
        state_ref[...] = state * cd_ref[...] + kv * (cd_ref[...] > 0.0).astype(F32)
        o_ref[rows, :] = _head_norm_gate(o, g_ref[rows, :], gmat_ref[...], norm_ref[...])


def _retention(q, k, v, gate, norm, batch, rows_per_step):
    t = q.shape[0]
    seq = t // batch
    steps = seq // rows_per_step
    log_g = np.log1p(-np.power(2.0, -5.0 - np.arange(N_HEADS, dtype=np.float64)))
    idx = np.arange(CHUNK, dtype=np.float64)
    diff = idx[:, None] - idx[None, :]
    intra = np.where(diff >= 0, np.exp(log_g[:, None, None] * np.maximum(diff, 0.0)), 0.0)
    intra = intra.reshape(N_HEADS * CHUNK, CHUNK)
    lane_g = np.repeat(log_g, HEAD_DIM)[None, :]
    xi = np.exp(lane_g * (idx[:, None] + 1.0))
    zeta = np.exp(lane_g * (CHUNK - 1.0 - idx[:, None]))
    head = np.arange(GROUP) // HEAD_DIM
    cd = (head[:, None] == head[None, :]) * np.exp(lane_g * CHUNK)
    consts = [jnp.asarray(a, F32) for a in (intra, xi, zeta, cd)]
    row = pl.BlockSpec((rows_per_step, GROUP), lambda b, i: (b * steps + i, 0))
    return pl.pallas_call(
        functools.partial(_ret_body, n_chunks=rows_per_step // CHUNK),
        grid=(batch, steps),
        in_specs=[row] * 4 + [_const_spec(c.shape) for c in consts] + [_const_spec((GROUP, GROUP)), _const_spec((1, GROUP))],
        out_specs=row,
        out_shape=jax.ShapeDtypeStruct((t, GROUP), BF16),
        scratch_shapes=[pltpu.VMEM((GROUP, GROUP), F32)],
        compiler_params=_params("parallel", "arbitrary"),
        name="retention",
    )(q, k, v, gate, *consts, _block_diag(BF16), norm.reshape(1, GROUP))


def _gla_body(q_ref, k_ref, v_ref, a_ref, g_ref, ltri_ref, lmask_ref, gmat_ref, bd_ref, norm_ref,
              o_ref, state_ref, *, n_chunks):
    @pl.when(pl.program_id(1) == 0)
    def _():
        state_ref[...] = jnp.zeros_like(state_ref)

    masks = _head_masks((CHUNK, GROUP))
    row = lax.broadcasted_iota(jnp.int32, (CHUNK, GROUP), 0)
    gmat = gmat_ref[...]
    ltri = ltri_ref[...]
    for c in range(n_chunks):
        rows = slice(c * CHUNK, (c + 1) * CHUNK)
        q = q_ref[rows, :]
        k = k_ref[rows, :]
        vb = v_ref[rows, :]
        v = vb.astype(F32)
        a = a_ref[rows, :]
        a0 = a.astype(BF16)
        r1 = a - a0.astype(F32)
        a1 = r1.astype(BF16)
        a2 = (r1 - a1.astype(F32)).astype(BF16)
        b = _dot(ltri, a0) + _dot(ltri, a1) + _dot(ltri, a2)
        b_last = b[CHUNK - 1:CHUNK, :]

        state = state_ref[...]
        o = lax.dot_general((q * jnp.exp(b)).astype(BF16), state.astype(BF16), NT_DIMS, preferred_element_type=F32)
        kd = (k * jnp.exp(b_last - b)).astype(BF16)
        kv = lax.dot_general(vb, kd, TN_DIMS, preferred_element_type=F32)
        state_ref[...] = state * jnp.exp(b_last) + kv * bd_ref[...]

        sc = None
        for li, m in enumerate(GLA_LEVELS):
            bref = jnp.concatenate(
                [jnp.broadcast_to(b[(2 * j + 1) * m - 1:(2 * j + 1) * m, :], (2 * m, GROUP)) for j in range(CHUNK // (2 * m))],
                axis=0)
            upper = ((row >> int(math.log2(m))) & 1) == 1
            w = jnp.exp(jnp.where(upper, b - bref, bref - b))
            qt = jnp.where(upper, q * w, 0.0).astype(BF16)
            kt = jnp.where(upper, 0.0, k * w).astype(BF16)
            term = lax.dot_general(_stack_heads(qt, masks), kt, NT_DIMS, preferred_element_type=F32) * lmask_ref[li]
            sc = term if sc is None else sc + term
        o = o + _unstack_heads(_dot(sc.astype(BF16), vb), masks, CHUNK)

        prods = [(q * k).astype(BF16)]
        for d in range(1, LEAF):
            valid = (row & (LEAF - 1)) >= d
            w = jnp.exp(jnp.where(valid, b - pltpu.roll(b, d, 0), NEG))
            prods.append((q * pltpu.roll(k, d, 0) * w).astype(BF16))
        sums = _dot(jnp.concatenate(prods, axis=0), gmat)
        o = o + sums[0:CHUNK] * v
        for d in range(1, LEAF):
            o = o + sums[d * CHUNK:(d + 1) * CHUNK] * pltpu.roll(v, d, 0)

        o_ref[rows, :] = _head_norm_gate(o, g_ref[rows, :], gmat, norm_ref[...])


def _gla(q, k, v, log_a, gate, norm, batch, rows_per_step):
    t = q.shape[0]
    seq = t // batch
    steps = seq // rows_per_step
    idx = np.arange(CHUNK)
    ltri = jnp.asarray(idx[:, None] >= idx[None, :], BF16)
    lm = []
    for m in GLA_LEVELS:
        same = (idx[:, None] // (2 * m)) == (idx[None, :] // (2 * m))
        ok = same & (((idx[:, None] // m) & 1) == 1) & (((idx[None, :] // m) & 1) == 0)
        lm.append(np.tile(ok, (N_HEADS, 1)))
    lmask = jnp.asarray(np.stack(lm), F32)
    row = pl.BlockSpec((rows_per_step, GROUP), lambda b, i: (b * steps + i, 0))
    return pl.pallas_call(
        functools.partial(_gla_body, n_chunks=rows_per_step // CHUNK),
        grid=(batch, steps),
        in_specs=[row] * 5 + [_const_spec((CHUNK, CHUNK)), _const_spec(lmask.shape), _const_spec((GROUP, GROUP)),
                              _const_spec((GROUP, GROUP)), _const_spec((1, GROUP))],
        out_specs=row,
        out_shape=jax.ShapeDtypeStruct((t, GROUP), BF16),
        scratch_shapes=[pltpu.VMEM((GROUP, GROUP), F32)],
        compiler_params=_params("parallel", "arbitrary"),
        name="gla",
    )(q, k, v, log_a, gate, ltri, lmask, _block_diag(BF16), _block_diag(F32), norm.reshape(1, GROUP))


def _mla_body(q_ref, k_ref, v_ref, o_ref, s0_ref, s1_ref, bm0_ref, bm1_ref, m_ref, acc_ref, *, tq):
    i = pl.program_id(2)
    q = q_ref[...]
    slots = ((s0_ref, bm0_ref), (s1_ref, bm1_ref))

    def scores(blk):
        start = pl.multiple_of(blk * tq, tq)
        return lax.dot_general(q, k_ref[pl.ds(start, tq), :], NT_DIMS, preferred_element_type=F32)

    def stash(slot, s):
        s_ref, bm_ref = slots[slot]
        s_ref[...] = s
        bm_ref[...] = jnp.max(s, axis=-1, keepdims=True)

    def absorb(slot, blk):
        s_ref, bm_ref = slots[slot]
        start = pl.multiple_of(blk * tq, tq)
        m_prev = m_ref[...]
        m_new = jnp.maximum(m_prev, bm_ref[...])
        p = jnp.exp2(s_ref[...] - m_new).astype(BF16)
        acc_ref[...] = acc_ref[...] * jnp.exp2(m_prev - m_new) + _dot(p, v_ref[pl.ds(start, tq), :])
        m_ref[...] = m_new

    m_ref[...] = jnp.full(m_ref.shape, NEG, F32)
    acc_ref[...] = jnp.zeros_like(acc_ref)
    s = scores(i)
    r = lax.broadcasted_iota(jnp.int32, s.shape, 0)
    cidx = lax.broadcasted_iota(jnp.int32, s.shape, 1)
    stash(0, jnp.where(cidx <= r, s, NEG))

    def body(j, carry):
        prev_blk = jnp.where(j == 0, i, j - 1)
        for slot in (0, 1):
            @pl.when((j & 1) == slot)
            def _():
                stash(1 - slot, scores(j))
                absorb(slot, prev_blk)
        return carry

    lax.fori_loop(0, i, body, 0)
    last_blk = jnp.where(i == 0, i, i - 1)
    for slot in (0, 1):
        @pl.when((i & 1) == slot)
        def _():
            absorb(slot, last_blk)
    acc = acc_ref[...]
    o_ref[...] = (acc[:, :MLA_DV] / acc[:, MLA_DV:MLA_DV + 1]).astype(BF16)


def _mla_attention(q, k, v, batch, tq):
    t = q.shape[0]
    seq = t // batch
    nq = seq // tq
    q3, k3, v3 = (a.reshape(batch, seq, N_HEADS * MLA_SLOT) for a in (q, k, v))
    kv_spec = pl.BlockSpec((None, seq, MLA_SLOT), lambda b, h, i: (b, 0, h))
    out = pl.pallas_call(
        functools.partial(_mla_body, tq=tq),
        grid=(batch, N_HEADS, nq),
        in_specs=[pl.BlockSpec((None, tq, MLA_SLOT), lambda b, h, i: (b, i, h)), kv_spec, kv_spec],
        out_specs=pl.BlockSpec((None, tq, MLA_DV), lambda b, h, i: (b, i, h)),
        out_shape=jax.ShapeDtypeStruct((batch, seq, N_HEADS * MLA_DV), BF16),
        scratch_shapes=[pltpu.VMEM((tq, tq), F32), pltpu.VMEM((tq, tq), F32), pltpu.VMEM((tq, 1), F32),
                        pltpu.VMEM((tq, 1), F32), pltpu.VMEM((tq, 1), F32), pltpu.VMEM((tq, MLA_SLOT), F32)],
        compiler_params=_params("parallel", "parallel", "arbitrary"),
        name="mla_attention",
    )(q3, k3, v3)
    return out.reshape(t, N_HEADS * MLA_DV)


def _outproj_body(x_ref, r_ref, m_ref, g_ref, wr_ref, wm_ref, wg_ref, o_ref, *, d_model):
    r = r_ref[...]
    m = m_ref[...]
    g = g_ref[...]
    for c in range(d_model // MXU_N):
        cols = slice(c * MXU_N, (c + 1) * MXU_N)
        o_ref[:, cols] = x_ref[:, cols] + (_dot(r, wr_ref[:, cols]) + _dot(m, wm_ref[:, cols]) + _dot(g, wg_ref[:, cols]))


def _outproj(x, ret, mla, gla, w_out, tm):
    t, d = x.shape
    nr, nm, ng = ret.shape[1], mla.shape[1], gla.shape[1]
    w = w_out.astype(BF16)
    row = lambda n: pl.BlockSpec((tm, n), lambda i: (i, 0))
    return pl.pallas_call(
        functools.partial(_outproj_body, d_model=d),
        grid=(t // tm,),
        in_specs=[row(d), row(nr), row(nm), row(ng), _const_spec((nr, d)), _const_spec((nm, d)), _const_spec((ng, d))],
        out_specs=row(d),
        out_shape=jax.ShapeDtypeStruct((t, d), F32),
        compiler_params=_params("parallel"),
        name="outproj",
    )(x, ret, mla, gla, w[:nr], w[nr:nr + nm], w[nr + nm:])


def _tiles(batch, seq):
    tm = min(512, seq)
    rows_per_step = min(1024, seq)
    tq = min(1024, seq)
    return tm, rows_per_step, tq


def kernel(x, positions, ffn1_norm, ffn1_w_gate_up, ffn1_w_down, mix_norm, w_in, ret_out_norm, mla_q_norm, mla_w_uq, mla_kv_norm, mla_w_ukv, mla_q_nope_norm, mla_q_rope_norm, mla_k_nope_norm, mla_k_rope_norm, gla_w_gate_up, gla_gate_bias, gla_out_norm, w_out, ffn2_norm, ffn2_w_gate_up, ffn2_w_down):
    batch, seq, d_model = x.shape
    depth = w_in.shape[0]
    tm, rows_per_step, tq = _tiles(batch, seq)
    xt = x.reshape(batch * seq, d_model)
    cos, sin = _rope_tables(positions, tm)
    for l in range(depth):
        xt = _ffn(xt, ffn1_norm[l], ffn1_w_gate_up[l], ffn1_w_down[l], tm)
        (rq, rk, rv, rg, mq, mk, mv, gq, gk, gv, la, gg) = _inproj(
            xt, cos, sin, mix_norm[l], w_in[l], mla_q_norm[l], mla_w_uq[l], mla_kv_norm[l], mla_w_ukv[l],
            mla_q_nope_norm[l], mla_q_rope_norm[l], mla_k_nope_norm[l], mla_k_rope_norm[l],
            gla_w_gate_up[l], gla_gate_bias[l], tm)
        ret = _retention(rq, rk, rv, rg, ret_out_norm[l], batch, rows_per_step)
        mla = _mla_attention(mq, mk, mv, batch, tq)
        gla = _gla(gq, gk, gv, la, gg, gla_out_norm[l], batch, rows_per_step)
        xt = _outproj(xt, ret, mla, gla, w_out[l], tm)
        xt = _ffn(xt, ffn2_norm[l], ffn2_w_gate_up[l], ffn2_w_down[l], tm)
    return xt.reshape(batch, seq, d_model)
```

```python
import functools
import math

import numpy as np
import jax
import jax.numpy as jnp
from jax import lax
from jax.experimental import pallas as pl
from jax.experimental.pallas import tpu as pltpu

F32 = jnp.float32
BF16 = jnp.bfloat16

EPS = 1e-6
ROPE_THETA = 10000.0
HEAD_DIM = 64
N_HEADS = 4
GROUP = N_HEADS * HEAD_DIM
MLA_NOPE = 128
MLA_ROPE = 64
MLA_DV = 128
MLA_Q_RANK = 384
MLA_KV_RANK = 256
MLA_SLOT = 256
GLA_GATE_RANK = 16
GLA_TAU = 16.0
CHUNK = 128
LEAF = 1
GLA_LEVELS = (64, 32, 16, 8, 4, 2, 1)
NEG = -1e30

LANE = 128
MXU_N = 256
VMEM_LIMIT_BYTES = 56 * 1024 * 1024

NT_DIMS = (((1,), (1,)), ((), ()))
TN_DIMS = (((0,), (0,)), ((), ()))


def _params(*sem):
    return pltpu.CompilerParams(dimension_semantics=sem, vmem_limit_bytes=VMEM_LIMIT_BYTES)


def _const_spec(shape):
    nd = len(shape)
    return pl.BlockSpec(shape, lambda *_: (0,) * nd)


def _dot(a, b):
    return jnp.dot(a, b, preferred_element_type=F32)


def _rms(x, g):
    return x * lax.rsqrt(jnp.mean(x * x, axis=-1, keepdims=True) + EPS) * g


def _silu(x):
    return x * jax.nn.sigmoid(x)


def _rope_table_body(pos_ref, inv_ref, sign_ref, cos_ref, sin_ref):
    ang = pos_ref[...].astype(F32) * inv_ref[...]
    cos_ref[...] = jnp.cos(ang)
    sin_ref[...] = jnp.sin(ang) * sign_ref[...]


def _rope_tables(positions, tm):
    t = positions.size
    half = HEAD_DIM // 2
    inv = ROPE_THETA ** (-jnp.arange(0, HEAD_DIM, 2, dtype=F32) / HEAD_DIM)
    inv_row = jnp.tile(inv, LANE // half).reshape(1, LANE)
    sign_row = jnp.tile(jnp.concatenate([-jnp.ones(half, F32), jnp.ones(half, F32)]), LANE // HEAD_DIM).reshape(1, LANE)
    pos = positions.reshape(t, 1)
    row = pl.BlockSpec((tm, LANE), lambda i: (i, 0))
    return pl.pallas_call(
        _rope_table_body,
        grid=(t // tm,),
        in_specs=[pl.BlockSpec((tm, 1), lambda i: (i, 0)), _const_spec((1, LANE)), _const_spec((1, LANE))],
        out_specs=[row, row],
        out_shape=[jax.ShapeDtypeStruct((t, LANE), F32)] * 2,
        compiler_params=_params("parallel"),
        name="rope_tables",
    )(pos, inv_row, sign_row)


def _rope(x, cos, sin):
    w = x.shape[-1]
    reps = w // LANE
    if reps > 1:
        cos = jnp.concatenate([cos] * reps, axis=-1)
        sin = jnp.concatenate([sin] * reps, axis=-1)
    lane = lax.broadcasted_iota(jnp.int32, x.shape, 1)
    first_half = (lane & (HEAD_DIM // 2)) == 0
    swapped = jnp.where(first_half, pltpu.roll(x, w - HEAD_DIM // 2, 1), pltpu.roll(x, HEAD_DIM // 2, 1))
    return x * cos + swapped * sin


def _ffn_body(x_ref, g_ref, wgu_ref, wd_ref, o_ref, act_ref, *, d_ff, d_model):
    x = x_ref[...]
    xn = _rms(x, g_ref[...]).astype(BF16)
    for c in range(d_ff // MXU_N):
        cols = slice(c * MXU_N, (c + 1) * MXU_N)
        ucols = slice(d_ff + c * MXU_N, d_ff + (c + 1) * MXU_N)
        g = _dot(xn, wgu_ref[:, cols])
        u = _dot(xn, wgu_ref[:, ucols])
        act_ref[:, cols] = (_silu(g) * u).astype(BF16)
    act = act_ref[...]
    for c in range(d_model // MXU_N):
        cols = slice(c * MXU_N, (c + 1) * MXU_N)
        o_ref[:, cols] = x_ref[:, cols] + 0.5 * _dot(act, wd_ref[:, cols])


def _ffn(x, norm, w_gate_up, w_down, tm):
    t, d_model = x.shape
    d_ff = w_down.shape[0]
    row = pl.BlockSpec((tm, d_model), lambda i: (i, 0))
    return pl.pallas_call(
        functools.partial(_ffn_body, d_ff=d_ff, d_model=d_model),
        grid=(t // tm,),
        in_specs=[row, _const_spec((1, d_model)), _const_spec((d_model, 2 * d_ff)), _const_spec((d_ff, d_model))],
        out_specs=row,
        out_shape=jax.ShapeDtypeStruct((t, d_model), F32),
        scratch_shapes=[pltpu.VMEM((tm, d_ff), BF16)],
        compiler_params=_params("parallel"),
        name="ffn",
    )(x, norm.reshape(1, d_model), w_gate_up.astype(BF16), w_down.astype(BF16))


W_RET = 0
W_CQ = 4 * GROUP
W_TAIL = W_CQ + MLA_Q_RANK
W_CKV = W_TAIL + LANE
W_GLA = W_CKV + MLA_KV_RANK
W_GLA_R = W_GLA + 3 * GROUP
W_COLS = W_GLA_R + GROUP


def _inproj_body(x_ref, g_ref, cos_ref, sin_ref, w_ref, qn_ref, wuq_ref, kvn_ref, wukv_ref,
                 gqn_ref, gqr_ref, gkn_ref, gkr_ref, wga_ref, ba_ref,
                 rq_ref, rk_ref, rv_ref, rg_ref, mq_ref, mk_ref, mv_ref,
                 gq_ref, gk_ref, gv_ref, la_ref, gg_ref):
    cos = cos_ref[...]
    sin = sin_ref[...]
    xn = _rms(x_ref[...], g_ref[...]).astype(BF16)
    tm = xn.shape[0]
    lane = lax.broadcasted_iota(jnp.int32, (tm, LANE), 1)
    low_half = lane < HEAD_DIM
    dk_scale = HEAD_DIM ** -0.5

    def proj(lo, width):
        return _dot(xn, w_ref[:, lo:lo + width])

    rq_ref[...] = _rope(proj(W_RET, GROUP), cos, sin).astype(BF16)
    rk_ref[...] = (_rope(proj(W_RET + GROUP, GROUP), cos, sin) * dk_scale).astype(BF16)
    rv_ref[...] = proj(W_RET + 2 * GROUP, GROUP).astype(BF16)
    rg_ref[...] = _silu(proj(W_RET + 3 * GROUP, GROUP)).astype(BF16)

    gq_ref[...] = proj(W_GLA, GROUP) * dk_scale
    gk_ref[...] = proj(W_GLA + GROUP, GROUP)
    gv_ref[...] = proj(W_GLA + 2 * GROUP, GROUP).astype(BF16)
    gg_ref[...] = _silu(proj(W_GLA_R, GROUP)).astype(BF16)
    cq_tail = proj(W_CQ, MLA_Q_RANK + LANE)
    tail = cq_tail[:, MLA_Q_RANK:]
    z = _dot(tail.astype(BF16), wga_ref[...]) + ba_ref[...]
    la_ref[...] = (jnp.minimum(z, 0.0) - jnp.log(1.0 + jnp.exp(-jnp.abs(z)))) * (1.0 / GLA_TAU)

    kr_ms = jnp.sum(jnp.where(low_half, tail * tail, 0.0), axis=-1, keepdims=True) * (1.0 / MLA_ROPE)
    kr = _rope(tail * lax.rsqrt(kr_ms + EPS) * gkr_ref[...], cos, sin)
    kr_t = kr.T.astype(BF16)
    ckv = _rms(proj(W_CKV, MLA_KV_RANK), kvn_ref[...]).astype(BF16)
    ones_col = jnp.where(lane == 0, 1.0, 0.0).astype(BF16)
    v_base = N_HEADS * MLA_NOPE
    for pair in range(N_HEADS // 2):
        kn2 = _dot(ckv, wukv_ref[:, pair * MXU_N:(pair + 1) * MXU_N])
        v2 = _dot(ckv, wukv_ref[:, v_base + pair * MXU_N:v_base + (pair + 1) * MXU_N])
        for sub in range(2):
            h = 2 * pair + sub
            kn = kn2[:, sub * MLA_NOPE:(sub + 1) * MLA_NOPE]
            mk_ref[h, 0:MLA_NOPE, :] = _rms(kn, gkn_ref[...]).T.astype(BF16)
            mk_ref[h, MLA_NOPE:MLA_SLOT, :] = kr_t
            mv_ref[:, h * MLA_SLOT:h * MLA_SLOT + MLA_DV] = v2[:, sub * MLA_DV:(sub + 1) * MLA_DV].astype(BF16)
            mv_ref[:, h * MLA_SLOT + MLA_DV:(h + 1) * MLA_SLOT] = ones_col

    sm_scale = (MLA_NOPE + MLA_ROPE) ** -0.5 * math.log2(math.e)
    cq = _rms(cq_tail[:, :MLA_Q_RANK], qn_ref[...]).astype(BF16)
    for pair in range(N_HEADS // 2):
        qn2 = _dot(cq, wuq_ref[:, pair * MXU_N:(pair + 1) * MXU_N])
        for sub in range(2):
            h = 2 * pair + sub
            qn = qn2[:, sub * MLA_NOPE:(sub + 1) * MLA_NOPE]
            mq_ref[:, h * MLA_SLOT:h * MLA_SLOT + MLA_NOPE] = (_rms(qn, gqn_ref[...]) * sm_scale).astype(BF16)
    qr4 = _dot(cq, wuq_ref[:, N_HEADS * MLA_NOPE:N_HEADS * MLA_NOPE + N_HEADS * MLA_ROPE])
    for pair in range(N_HEADS // 2):
        qr = qr4[:, pair * LANE:(pair + 1) * LANE]
        q2 = qr * qr
        ms_lo = jnp.sum(jnp.where(low_half, q2, 0.0), axis=-1, keepdims=True)
        ms_hi = jnp.sum(jnp.where(low_half, 0.0, q2), axis=-1, keepdims=True)
        ms = jnp.where(low_half, ms_lo, ms_hi) * (1.0 / MLA_ROPE)
        qr = _rope(qr * lax.rsqrt(ms + EPS) * gqr_ref[...], cos, sin) * sm_scale
        h0 = 2 * pair
        mq_ref[:, h0 * MLA_SLOT + MLA_NOPE:(h0 + 1) * MLA_SLOT] = jnp.where(low_half, qr, 0.0).astype(BF16)
        mq_ref[:, (h0 + 1) * MLA_SLOT + MLA_NOPE:(h0 + 2) * MLA_SLOT] = jnp.where(
            low_half, pltpu.roll(qr, HEAD_DIM, 1), 0.0).astype(BF16)


def _reorder_w_in(w_in):
    d = w_in.shape[0]
    o = np.cumsum([0, GROUP, GROUP, GROUP, GROUP, MLA_Q_RANK, MLA_KV_RANK, MLA_ROPE,
                   GROUP, GROUP, GROUP, GLA_GATE_RANK, GROUP])
    seg = [w_in[:, o[i]:o[i + 1]] for i in range(12)]
    ret_q, ret_k, ret_v, ret_g, cq, ckv, kr, gq, gk, gv, a_low, gr = seg
    pad = jnp.zeros((d, LANE - MLA_ROPE - GLA_GATE_RANK), w_in.dtype)
    return jnp.concatenate([ret_q, ret_k, ret_v, ret_g, cq, kr, a_low, pad, ckv, gq, gk, gv, gr], axis=1)


def _inproj(x, cos, sin, norm, w_in, mla_q_norm, mla_w_uq, mla_kv_norm, mla_w_ukv,
            q_nope_norm, q_rope_norm, k_nope_norm, k_rope_norm, gla_w_gate_up, gla_gate_bias, batch, tm):
    t, d = x.shape
    seq = t // batch
    tiles_per_seq = seq // tm
    w = _reorder_w_in(w_in).astype(BF16)
    qd = MLA_NOPE + MLA_ROPE
    wuq = mla_w_uq.reshape(MLA_Q_RANK, N_HEADS, qd)
    wuq = jnp.concatenate([wuq[:, :, :MLA_NOPE].reshape(MLA_Q_RANK, -1),
                           wuq[:, :, MLA_NOPE:].reshape(MLA_Q_RANK, -1)], axis=1).astype(BF16)
    wukv = mla_w_ukv.reshape(MLA_KV_RANK, N_HEADS, MLA_NOPE + MLA_DV)
    wukv = jnp.concatenate([wukv[:, :, :MLA_NOPE].reshape(MLA_KV_RANK, -1),
                            wukv[:, :, MLA_NOPE:].reshape(MLA_KV_RANK, -1)], axis=1).astype(BF16)
    gqr = jnp.tile(q_rope_norm, LANE // MLA_ROPE).reshape(1, LANE)
    gkr = jnp.concatenate([k_rope_norm, jnp.zeros(LANE - MLA_ROPE, F32)]).reshape(1, LANE)
    wga = jnp.zeros((LANE, GROUP), F32).at[MLA_ROPE:MLA_ROPE + GLA_GATE_RANK].set(gla_w_gate_up).astype(BF16)

    row = lambda n: pl.BlockSpec((tm, n), lambda i: (i, 0))
    out = lambda n, dt: jax.ShapeDtypeStruct((t, n), dt)
    wide = N_HEADS * MLA_SLOT
    keys_t = pl.BlockSpec((None, N_HEADS, MLA_SLOT, tm), lambda i: (i // tiles_per_seq, 0, 0, i % tiles_per_seq))
    return pl.pallas_call(
        _inproj_body,
        grid=(t // tm,),
        in_specs=[row(d), _const_spec((1, d)), row(LANE), row(LANE), _const_spec((d, W_COLS)),
                  _const_spec((1, MLA_Q_RANK)), _const_spec(wuq.shape), _const_spec((1, MLA_KV_RANK)),
                  _const_spec(wukv.shape), _const_spec((1, MLA_NOPE)), _const_spec((1, LANE)),
                  _const_spec((1, MLA_NOPE)), _const_spec((1, LANE)), _const_spec((LANE, GROUP)),
                  _const_spec((1, GROUP))],
        out_specs=[row(GROUP)] * 4 + [row(wide), keys_t, row(wide)] + [row(GROUP)] * 5,
        out_shape=[out(GROUP, BF16)] * 4
                  + [out(wide, BF16), jax.ShapeDtypeStruct((batch, N_HEADS, MLA_SLOT, seq), BF16), out(wide, BF16)]
                  + [out(GROUP, F32), out(GROUP, F32), out(GROUP, BF16), out(GROUP, F32), out(GROUP, BF16)],
        compiler_params=_params("parallel"),
        name="inproj",
    )(x, norm.reshape(1, d), cos, sin, w, mla_q_norm.reshape(1, -1), wuq, mla_kv_norm.reshape(1, -1), wukv,
      q_nope_norm.reshape(1, -1), gqr, k_nope_norm.reshape(1, -1), gkr, wga, gla_gate_bias.reshape(1, -1))


def _head_masks(shape):
    lane = lax.broadcasted_iota(jnp.int32, shape, 1)
    return [(lane >> int(math.log2(HEAD_DIM))) == h for h in range(N_HEADS)]


def _stack_heads(x, stack_mask):
    return jnp.concatenate([x] * N_HEADS, axis=0) * stack_mask


def _unstack_heads(y, masks, c):
    out = y[(N_HEADS - 1) * c:N_HEADS * c]
    for h in range(N_HEADS - 2, -1, -1):
        out = jnp.where(masks[h], y[h * c:(h + 1) * c], out)
    return out


def _stack_mask_const():
    head = np.arange(GROUP) // HEAD_DIM
    block = np.repeat(np.arange(N_HEADS), CHUNK)
    return jnp.asarray((block[:, None] == head[None, :]).astype(np.float32), BF16)


def _head_norm_gate(o, gate, gmat, norm):
    ms = _dot((o * o).astype(BF16), gmat) * (1.0 / HEAD_DIM)
    return (gate.astype(F32) * (o * lax.rsqrt(ms + EPS) * norm)).astype(BF16)


def _block_diag(dtype):
    idx = np.arange(GROUP) // HEAD_DIM
    return jnp.asarray((idx[:, None] == idx[None, :]).astype(np.float32), dtype)


def _ret_body(q_ref, k_ref, v_ref, g_ref, intra_ref, xi_ref, zeta_ref, cd_ref, gmat_ref, smask_ref, norm_ref,
              o_ref, state_ref, *, n_chunks):
    @pl.when(pl.program_id(1) == 0)
    def _():
        state_ref[...] = jnp.zeros_like(state_ref)

    masks = _head_masks((CHUNK, GROUP))
    for c in range(n_chunks):
        rows = slice(c * CHUNK, (c + 1) * CHUNK)
        q = q_ref[rows, :]
        k = k_ref[rows, :]
        v = v_ref[rows, :]
        sc = lax.dot_general(_stack_heads(q, smask_ref[...]), k, NT_DIMS, preferred_element_type=F32) * intra_ref[...]
        o = _unstack_heads(_dot(sc.astype(BF16), v), masks, CHUNK)
        state = state_ref[...]
        o = o + _dot(q, state.astype(BF16)) * xi_ref[...]
        vz = (v.astype(F32) * zeta_ref[...]).astype(BF16)
        kv = lax.dot_general(k, vz, TN_DIMS, preferred_element_type=F32)
        state_ref[...] = state * cd_ref[...] + kv * (cd_ref[...] > 0.0).astype(F32)
        o_ref[rows, :] = _head_norm_gate(o, g_ref[rows, :], gmat_ref[...], norm_ref[...])


def _retention(q, k, v, gate, norm, batch, rows_per_step):
    t = q.shape[0]
    seq = t // batch
    steps = seq // rows_per_step
    log_g = np.log1p(-np.power(2.0, -5.0 - np.arange(N_HEADS, dtype=np.float64)))
    idx = np.arange(CHUNK, dtype=np.float64)
    diff = idx[:, None] - idx[None, :]
    intra = np.where(diff >= 0, np.exp(log_g[:, None, None] * np.maximum(diff, 0.0)), 0.0)
    intra = intra.reshape(N_HEADS * CHUNK, CHUNK)
    lane_g = np.repeat(log_g, HEAD_DIM)[None, :]
    xi = np.exp(lane_g * (idx[:, None] + 1.0))
    zeta = np.exp(lane_g * (CHUNK - 1.0 - idx[:, None]))
    head = np.arange(GROUP) // HEAD_DIM
    cd = (head[:, None] == head[None, :]) * np.exp(lane_g * CHUNK)
    consts = [jnp.asarray(a, F32) for a in (intra, xi, zeta, cd)]
    row = pl.BlockSpec((rows_per_step, GROUP), lambda b, i: (b * steps + i, 0))
    return pl.pallas_call(
        functools.partial(_ret_body, n_chunks=rows_per_step // CHUNK),
        grid=(batch, steps),
        in_specs=[row] * 4 + [_const_spec(c.shape) for c in consts]
                 + [_const_spec((GROUP, GROUP)), _const_spec((N_HEADS * CHUNK, GROUP)), _const_spec((1, GROUP))],
        out_specs=row,
        out_shape=jax.ShapeDtypeStruct((t, GROUP), BF16),
        scratch_shapes=[pltpu.VMEM((GROUP, GROUP), F32)],
        compiler_params=_params("parallel", "arbitrary"),
        name="retention",
    )(q, k, v, gate, *consts, _block_diag(BF16), _stack_mask_const(), norm.reshape(1, GROUP))


SUBLANES = 8


def _boundary_rows(b, m, row):
    if 2 * m >= SUBLANES:
        pieces = []
        for j in range(CHUNK // (2 * m)):
            r = (2 * j + 1) * m - 1
            pieces.append(jnp.broadcast_to(b[r:r + 1, :], (2 * m, GROUP)))
        return jnp.concatenate(pieces, axis=0)
    tiles = b.reshape(CHUNK // SUBLANES, SUBLANES, GROUP)
    sub = (row & (SUBLANES - 1)).reshape(tiles.shape)
    out = None
    for j in range(SUBLANES // (2 * m) - 1, -1, -1):
        r = (2 * j + 1) * m - 1
        piece = jnp.broadcast_to(tiles[:, r:r + 1, :], tiles.shape)
        out = piece if out is None else jnp.where(sub < (2 * j + 2) * m, piece, out)
    return out.reshape(CHUNK, GROUP)
def _gla_body(q_ref, k_ref, v_ref, a_ref, g_ref, ltri_ref, lmask_ref, gmat_ref, bd_ref, smask_ref, norm_ref,
              o_ref, state_ref, *, n_chunks):
    @pl.when(pl.program_id(1) == 0)
    def _():
        state_ref[...] = jnp.zeros_like(state_ref)

    masks = _head_masks((CHUNK, GROUP))
    row = lax.broadcasted_iota(jnp.int32, (CHUNK, GROUP), 0)
    gmat = gmat_ref[...]
    ltri = ltri_ref[...]
    for c in range(n_chunks):
        rows = slice(c * CHUNK, (c + 1) * CHUNK)
        q = q_ref[rows, :]
        k = k_ref[rows, :]
        vb = v_ref[rows, :]
        v = vb.astype(F32)
        a = a_ref[rows, :]
        a0 = a.astype(BF16)
        r1 = a - a0.astype(F32)
        a1 = r1.astype(BF16)
        a2 = (r1 - a1.astype(F32)).astype(BF16)
        b = _dot(ltri, a0) + _dot(ltri, a1) + _dot(ltri, a2)
        b_last = b[CHUNK - 1:CHUNK, :]

        state = state_ref[...]
        o = lax.dot_general((q * jnp.exp(b)).astype(BF16), state.astype(BF16), NT_DIMS, preferred_element_type=F32)
        kd = (k * jnp.exp(b_last - b)).astype(BF16)
        kv = lax.dot_general(vb, kd, TN_DIMS, preferred_element_type=F32)
        state_ref[...] = state * jnp.exp(b_last) + kv * bd_ref[...]

        sc = None
        for li, m in enumerate(GLA_LEVELS):
            bref = _boundary_rows(b, m, row)
            upper = ((row >> int(math.log2(m))) & 1) == 1
            w = jnp.exp(jnp.where(upper, b - bref, bref - b))
            qt = jnp.where(upper, q * w, 0.0).astype(BF16)
            kt = jnp.where(upper, 0.0, k * w).astype(BF16)
            term = lax.dot_general(_stack_heads(qt, smask_ref[...]), kt, NT_DIMS, preferred_element_type=F32) * lmask_ref[li]
            sc = term if sc is None else sc + term
        o = o + _unstack_heads(_dot(sc.astype(BF16), vb), masks, CHUNK)

        qk = q * k
        qk_hi = qk.astype(BF16)
        prods = [qk_hi, (qk - qk_hi.astype(F32)).astype(BF16)]
        for d in range(1, LEAF):
            valid = (row & (LEAF - 1)) >= d
            w = jnp.exp(jnp.where(valid, b - pltpu.roll(b, d, 0), NEG))
            prods.append((q * pltpu.roll(k, d, 0) * w).astype(BF16))
        sums = _dot(jnp.concatenate(prods, axis=0), gmat)
        o = o + (sums[0:CHUNK] + sums[CHUNK:2 * CHUNK]) * v
        for d in range(1, LEAF):
            o = o + sums[(d + 1) * CHUNK:(d + 2) * CHUNK] * pltpu.roll(v, d, 0)

        o_ref[rows, :] = _head_norm_gate(o, g_ref[rows, :], gmat, norm_ref[...])


def _gla(q, k, v, log_a, gate, norm, batch, rows_per_step):
    t = q.shape[0]
    seq = t // batch
    steps = seq // rows_per_step
    idx = np.arange(CHUNK)
    ltri = jnp.asarray(idx[:, None] >= idx[None, :], BF16)
    lm = []
    for m in GLA_LEVELS:
        same = (idx[:, None] // (2 * m)) == (idx[None, :] // (2 * m))
        ok = same & (((idx[:, None] // m) & 1) == 1) & (((idx[None, :] // m) & 1) == 0)
        lm.append(np.tile(ok, (N_HEADS, 1)))
    lmask = jnp.asarray(np.stack(lm), F32)
    row = pl.BlockSpec((rows_per_step, GROUP), lambda b, i: (b * steps + i, 0))
    return pl.pallas_call(
        functools.partial(_gla_body, n_chunks=rows_per_step // CHUNK),
        grid=(batch, steps),
        in_specs=[row] * 5 + [_const_spec((CHUNK, CHUNK)), _const_spec(lmask.shape), _const_spec((GROUP, GROUP)),
                              _const_spec((GROUP, GROUP)), _const_spec((N_HEADS * CHUNK, GROUP)), _const_spec((1, GROUP))],
        out_specs=row,
        out_shape=jax.ShapeDtypeStruct((t, GROUP), BF16),
        scratch_shapes=[pltpu.VMEM((GROUP, GROUP), F32)],
        compiler_params=_params("parallel", "arbitrary"),
        name="gla",
    )(q, k, v, log_a, gate, ltri, lmask, _block_diag(BF16), _block_diag(F32), _stack_mask_const(), norm.reshape(1, GROUP))


def _mla_body(q_ref, k_ref, v_ref, o_ref, s0_ref, s1_ref, bm0_ref, bm1_ref, m_ref, acc_ref, *, tq):
    i = pl.program_id(2)
    q = q_ref[...]
    slots = ((s0_ref, bm0_ref), (s1_ref, bm1_ref))

    def scores(blk):
        start = pl.multiple_of(blk * tq, tq)
        return _dot(q, k_ref[:, pl.ds(start, tq)])

    def stash(slot, s):
        s_ref, bm_ref = slots[slot]
        s_ref[...] = s
        bm_ref[...] = jnp.max(s, axis=-1, keepdims=True)

    def absorb(slot, blk):
        s_ref, bm_ref = slots[slot]
        start = pl.multiple_of(blk * tq, tq)
        m_prev = m_ref[...]
        m_new = jnp.maximum(m_prev, bm_ref[...])
        p = jnp.exp2(s_ref[...] - m_new).astype(BF16)
        acc_ref[...] = acc_ref[...] * jnp.exp2(m_prev - m_new) + _dot(p, v_ref[pl.ds(start, tq), :])
        m_ref[...] = m_new

    m_ref[...] = jnp.full(m_ref.shape, NEG, F32)
    acc_ref[...] = jnp.zeros_like(acc_ref)
    s = scores(i)
    r = lax.broadcasted_iota(jnp.int32, s.shape, 0)
    cidx = lax.broadcasted_iota(jnp.int32, s.shape, 1)
    stash(0, jnp.where(cidx <= r, s, NEG))

    def body(j, carry):
        prev_blk = jnp.where(j == 0, i, j - 1)
        for slot in (0, 1):
            @pl.when((j & 1) == slot)
            def _():
                stash(1 - slot, scores(j))
                absorb(slot, prev_blk)
        return carry

    lax.fori_loop(0, i, body, 0)
    last_blk = jnp.where(i == 0, i, i - 1)
    for slot in (0, 1):
        @pl.when((i & 1) == slot)
        def _():
            absorb(slot, last_blk)
    acc = acc_ref[...]
    o_ref[...] = (acc[:, :MLA_DV] / acc[:, MLA_DV:MLA_DV + 1]).astype(BF16)


def _mla_attention(q, k, v, batch, tq):
    t = q.shape[0]
    seq = t // batch
    nq = seq // tq
    q3, v3 = (a.reshape(batch, seq, N_HEADS * MLA_SLOT) for a in (q, v))
    k_spec = pl.BlockSpec((None, None, MLA_SLOT, seq), lambda b, h, i: (b, h, 0, 0))
    v_spec = pl.BlockSpec((None, seq, MLA_SLOT), lambda b, h, i: (b, 0, h))
    out = pl.pallas_call(
        functools.partial(_mla_body, tq=tq),
        grid=(batch, N_HEADS, nq),
        in_specs=[pl.BlockSpec((None, tq, MLA_SLOT), lambda b, h, i: (b, i, h)), k_spec, v_spec],
        out_specs=pl.BlockSpec((None, tq, MLA_DV), lambda b, h, i: (b, i, h)),
        out_shape=jax.ShapeDtypeStruct((batch, seq, N_HEADS * MLA_DV), BF16),
        scratch_shapes=[pltpu.VMEM((tq, tq), F32), pltpu.VMEM((tq, tq), F32), pltpu.VMEM((tq, 1), F32),
                        pltpu.VMEM((tq, 1), F32), pltpu.VMEM((tq, 1), F32), pltpu.VMEM((tq, MLA_SLOT), F32)],
        compiler_params=_params("parallel", "parallel", "arbitrary"),
        name="mla_attention",
    )(q3, k, v3)
    return out.reshape(t, N_HEADS * MLA_DV)


def _outproj_body(x_ref, r_ref, m_ref, g_ref, wr_ref, wm_ref, wg_ref, o_ref, *, d_model):
    r = r_ref[...]
    m = m_ref[...]
    g = g_ref[...]
    for c in range(d_model // MXU_N):
        cols = slice(c * MXU_N, (c + 1) * MXU_N)
        o_ref[:, cols] = x_ref[:, cols] + (_dot(r, wr_ref[:, cols]) + _dot(m, wm_ref[:, cols]) + _dot(g, wg_ref[:, cols]))


def _outproj(x, ret, mla, gla, w_out, tm):
    t, d = x.shape
    nr, nm, ng = ret.shape[1], mla.shape[1], gla.shape[1]
    w = w_out.astype(BF16)
    row = lambda n: pl.BlockSpec((tm, n), lambda i: (i, 0))
    return pl.pallas_call(
        functools.partial(_outproj_body, d_model=d),
        grid=(t // tm,),
        in_specs=[row(d), row(nr), row(nm), row(ng), _const_spec((nr, d)), _const_spec((nm, d)), _const_spec((ng, d))],
        out_specs=row(d),
        out_shape=jax.ShapeDtypeStruct((t, d), F32),
        compiler_params=_params("parallel"),
        name="outproj",
    )(x, ret, mla, gla, w[:nr], w[nr:nr + nm], w[nr + nm:])


def _tiles(batch, seq):
    tm = min(512, seq)
    rows_per_step = min(1024, seq)
    tq = min(1024, seq)
    return tm, rows_per_step, tq


def kernel(x, positions, ffn1_norm, ffn1_w_gate_up, ffn1_w_down, mix_norm, w_in, ret_out_norm, mla_q_norm, mla_w_uq, mla_kv_norm, mla_w_ukv, mla_q_nope_norm, mla_q_rope_norm, mla_k_nope_norm, mla_k_rope_norm, gla_w_gate_up, gla_gate_bias, gla_out_norm, w_out, ffn2_norm, ffn2_w_gate_up, ffn2_w_down):
    batch, seq, d_model = x.shape
    depth = w_in.shape[0]
    tm, rows_per_step, tq = _tiles(batch, seq)
    xt = x.reshape(batch * seq, d_model)
    cos, sin = _rope_tables(positions, tm)
    for l in range(depth):
        xt = _ffn(xt, ffn1_norm[l], ffn1_w_gate_up[l], ffn1_w_down[l], tm)
        (rq, rk, rv, rg, mq, mk, mv, gq, gk, gv, la, gg) = _inproj(
            xt, cos, sin, mix_norm[l], w_in[l], mla_q_norm[l], mla_w_uq[l], mla_kv_norm[l], mla_w_ukv[l],
            mla_q_nope_norm[l], mla_q_rope_norm[l], mla_k_nope_norm[l], mla_k_rope_norm[l],
            gla_w_gate_up[l], gla_gate_bias[l], batch, tm)
        ret = _retention(rq, rk, rv, rg, ret_out_norm[l], batch, rows_per_step)
        mla = _mla_attention(mq, mk, mv, batch, tq)
        gla = _gla(gq, gk, gv, la, gg, gla_out_norm[l], batch, rows_per_step)
        xt = _outproj(xt, ret, mla, gla, w_out[l], tm)
        xt = _ffn(xt, ffn2_norm[l], ffn2_w_gate_up[l], ffn2_w_down[l], tm)
    return xt.reshape(batch, seq, d_model)
```

```python
import functools
import math

import numpy as np
import jax
import jax.numpy as jnp
from jax import lax
from jax.experimental import pallas as pl
from jax.experimental.pallas import tpu as pltpu

F32 = jnp.float32
BF16 = jnp.bfloat16

EPS = 1e-6
ROPE_THETA = 10000.0
HEAD_DIM = 64
N_HEADS = 4
GROUP = N_HEADS * HEAD_DIM
MLA_NOPE = 128
MLA_ROPE = 64
MLA_DV = 128
MLA_Q_RANK = 384
MLA_KV_RANK = 256
MLA_SLOT = 256
GLA_GATE_RANK = 16
GLA_TAU = 16.0
CHUNK = 128
LEAF = 1
GLA_LEVELS = (64, 32, 16, 8, 4, 2, 1)
NEG = -1e30

LANE = 128
MXU_N = 256
VMEM_LIMIT_BYTES = 56 * 1024 * 1024

NT_DIMS = (((1,), (1,)), ((), ()))
TN_DIMS = (((0,), (0,)), ((), ()))


def _params(*sem):
    return pltpu.CompilerParams(dimension_semantics=sem, vmem_limit_bytes=VMEM_LIMIT_BYTES)


def _const_spec(shape):
    nd = len(shape)
    return pl.BlockSpec(shape, lambda *_: (0,) * nd)


def _dot(a, b):
    return jnp.dot(a, b, preferred_element_type=F32)


def _rms(x, g):
    return x * lax.rsqrt(jnp.mean(x * x, axis=-1, keepdims=True) + EPS) * g


def _silu(x):
    return x * jax.nn.sigmoid(x)


def _rope_table_body(pos_ref, inv_ref, sign_ref, cos_ref, sin_ref):
    ang = pos_ref[...].astype(F32) * inv_ref[...]
    cos_ref[...] = jnp.cos(ang)
    sin_ref[...] = jnp.sin(ang) * sign_ref[...]


def _rope_tables(positions, tm):
    t = positions.size
    half = HEAD_DIM // 2
    inv = ROPE_THETA ** (-jnp.arange(0, HEAD_DIM, 2, dtype=F32) / HEAD_DIM)
    inv_row = jnp.tile(inv, LANE // half).reshape(1, LANE)
    sign_row = jnp.tile(jnp.concatenate([-jnp.ones(half, F32), jnp.ones(half, F32)]), LANE // HEAD_DIM).reshape(1, LANE)
    pos = positions.reshape(t, 1)
    row = pl.BlockSpec((tm, LANE), lambda i: (i, 0))
    return pl.pallas_call(
        _rope_table_body,
        grid=(t // tm,),
        in_specs=[pl.BlockSpec((tm, 1), lambda i: (i, 0)), _const_spec((1, LANE)), _const_spec((1, LANE))],
        out_specs=[row, row],
        out_shape=[jax.ShapeDtypeStruct((t, LANE), F32)] * 2,
        compiler_params=_params("parallel"),
        name="rope_tables",
    )(pos, inv_row, sign_row)


def _rope(x, cos, sin):
    w = x.shape[-1]
    reps = w // LANE
    if reps > 1:
        cos = jnp.concatenate([cos] * reps, axis=-1)
        sin = jnp.concatenate([sin] * reps, axis=-1)
    lane = lax.broadcasted_iota(jnp.int32, x.shape, 1)
    first_half = (lane & (HEAD_DIM // 2)) == 0
    swapped = jnp.where(first_half, pltpu.roll(x, w - HEAD_DIM // 2, 1), pltpu.roll(x, HEAD_DIM // 2, 1))
    return x * cos + swapped * sin


def _ffn_body(*refs, d_ff, d_model, with_mix):
    if with_mix:
        x_ref, r_ref, m_ref, a_ref, wr_ref, wm_ref, wa_ref, g_ref, wgu_ref, wd_ref, o_ref, act_ref, x_mid_ref = refs
        r, m, a = r_ref[...], m_ref[...], a_ref[...]
        for c in range(d_model // MXU_N):
            cols = slice(c * MXU_N, (c + 1) * MXU_N)
            x_mid_ref[:, cols] = x_ref[:, cols] + (
                _dot(r, wr_ref[:, cols]) + _dot(m, wm_ref[:, cols]) + _dot(a, wa_ref[:, cols]))
        x_ref = x_mid_ref
    else:
        x_ref, g_ref, wgu_ref, wd_ref, o_ref, act_ref = refs
    xn = _rms(x_ref[...], g_ref[...]).astype(BF16)
    for c in range(d_ff // MXU_N):
        cols = slice(c * MXU_N, (c + 1) * MXU_N)
        ucols = slice(d_ff + c * MXU_N, d_ff + (c + 1) * MXU_N)
        g = _dot(xn, wgu_ref[:, cols])
        u = _dot(xn, wgu_ref[:, ucols])
        act_ref[:, cols] = (_silu(g) * u).astype(BF16)
    act = act_ref[...]
    for c in range(d_model // MXU_N):
        cols = slice(c * MXU_N, (c + 1) * MXU_N)
        o_ref[:, cols] = x_ref[:, cols] + 0.5 * _dot(act, wd_ref[:, cols])


def _ffn(x, norm, w_gate_up, w_down, tm, mix=None):
    t, d_model = x.shape
    d_ff = w_down.shape[0]
    row = lambda n: pl.BlockSpec((tm, n), lambda i: (i, 0))
    operands, specs, scratch = [x], [row(d_model)], [pltpu.VMEM((tm, d_ff), BF16)]
    if mix is not None:
        ret, mla, gla, w_out = mix
        nr, nm, na = ret.shape[1], mla.shape[1], gla.shape[1]
        w = w_out.astype(BF16)
        operands += [ret, mla, gla, w[:nr], w[nr:nr + nm], w[nr + nm:]]
        specs += [row(nr), row(nm), row(na), _const_spec((nr, d_model)), _const_spec((nm, d_model)),
                  _const_spec((na, d_model))]
        scratch.append(pltpu.VMEM((tm, d_model), F32))
    operands += [norm.reshape(1, d_model), w_gate_up.astype(BF16), w_down.astype(BF16)]
    specs += [_const_spec((1, d_model)), _const_spec((d_model, 2 * d_ff)), _const_spec((d_ff, d_model))]
    return pl.pallas_call(
        functools.partial(_ffn_body, d_ff=d_ff, d_model=d_model, with_mix=mix is not None),
        grid=(t // tm,),
        in_specs=specs,
        out_specs=row(d_model),
        out_shape=jax.ShapeDtypeStruct((t, d_model), F32),
        scratch_shapes=scratch,
        compiler_params=_params("parallel"),
        name="mix_ffn" if mix is not None else "ffn",
    )(*operands)


W_RET = 0
W_CQ = 4 * GROUP
W_TAIL = W_CQ + MLA_Q_RANK
W_CKV = W_TAIL + LANE
W_GLA = W_CKV + MLA_KV_RANK
W_GLA_R = W_GLA + 3 * GROUP
W_COLS = W_GLA_R + GROUP


def _inproj_body(x_ref, g_ref, cos_ref, sin_ref, w_ref, qn_ref, wuq_ref, kvn_ref, wukv_ref,
                 gqn_ref, gqr_ref, gkn_ref, gkr_ref, wga_ref, ba_ref,
                 rq_ref, rk_ref, rv_ref, rg_ref, mq_ref, mk_ref, mv_ref,
                 gq_ref, gk_ref, gv_ref, la_ref, gg_ref):
    cos = cos_ref[...]
    sin = sin_ref[...]
    xn = _rms(x_ref[...], g_ref[...]).astype(BF16)
    tm = xn.shape[0]
    lane = lax.broadcasted_iota(jnp.int32, (tm, LANE), 1)
    low_half = lane < HEAD_DIM
    dk_scale = HEAD_DIM ** -0.5

    def proj(lo, width):
        return _dot(xn, w_ref[:, lo:lo + width])

    rq_ref[...] = _rope(proj(W_RET, GROUP), cos, sin).astype(BF16)
    rk_ref[...] = (_rope(proj(W_RET + GROUP, GROUP), cos, sin) * dk_scale).astype(BF16)
    rv_ref[...] = proj(W_RET + 2 * GROUP, GROUP).astype(BF16)
    rg_ref[...] = _silu(proj(W_RET + 3 * GROUP, GROUP)).astype(BF16)

    gq_ref[...] = proj(W_GLA, GROUP) * dk_scale
    gk_ref[...] = proj(W_GLA + GROUP, GROUP)
    gv_ref[...] = proj(W_GLA + 2 * GROUP, GROUP).astype(BF16)
    gg_ref[...] = _silu(proj(W_GLA_R, GROUP)).astype(BF16)
    cq_tail = proj(W_CQ, MLA_Q_RANK + LANE)
    tail = cq_tail[:, MLA_Q_RANK:]
    z = _dot(tail.astype(BF16), wga_ref[...]) + ba_ref[...]
    la_ref[...] = (jnp.minimum(z, 0.0) - jnp.log(1.0 + jnp.exp(-jnp.abs(z)))) * (1.0 / GLA_TAU)

    kr_ms = jnp.sum(jnp.where(low_half, tail * tail, 0.0), axis=-1, keepdims=True) * (1.0 / MLA_ROPE)
    kr = _rope(tail * lax.rsqrt(kr_ms + EPS) * gkr_ref[...], cos, sin)
    kr_t = kr.T.astype(BF16)
    ckv = _rms(proj(W_CKV, MLA_KV_RANK), kvn_ref[...]).astype(BF16)
    ones_col = jnp.where(lane == 0, 1.0, 0.0).astype(BF16)
    v_base = N_HEADS * MLA_NOPE
    for pair in range(N_HEADS // 2):
        kn2 = _dot(ckv, wukv_ref[:, pair * MXU_N:(pair + 1) * MXU_N])
        v2 = _dot(ckv, wukv_ref[:, v_base + pair * MXU_N:v_base + (pair + 1) * MXU_N])
        for sub in range(2):
            h = 2 * pair + sub
            kn = kn2[:, sub * MLA_NOPE:(sub + 1) * MLA_NOPE]
            mk_ref[h, 0:MLA_NOPE, :] = _rms(kn, gkn_ref[...]).T.astype(BF16)
            mk_ref[h, MLA_NOPE:MLA_SLOT, :] = kr_t
            mv_ref[:, h * MLA_SLOT:h * MLA_SLOT + MLA_DV] = v2[:, sub * MLA_DV:(sub + 1) * MLA_DV].astype(BF16)
            mv_ref[:, h * MLA_SLOT + MLA_DV:(h + 1) * MLA_SLOT] = ones_col

    sm_scale = (MLA_NOPE + MLA_ROPE) ** -0.5 * math.log2(math.e)
    cq = _rms(cq_tail[:, :MLA_Q_RANK], qn_ref[...]).astype(BF16)
    for pair in range(N_HEADS // 2):
        qn2 = _dot(cq, wuq_ref[:, pair * MXU_N:(pair + 1) * MXU_N])
        for sub in range(2):
            h = 2 * pair + sub
            qn = qn2[:, sub * MLA_NOPE:(sub + 1) * MLA_NOPE]
            mq_ref[:, h * MLA_SLOT:h * MLA_SLOT + MLA_NOPE] = (_rms(qn, gqn_ref[...]) * sm_scale).astype(BF16)
    qr4 = _dot(cq, wuq_ref[:, N_HEADS * MLA_NOPE:N_HEADS * MLA_NOPE + N_HEADS * MLA_ROPE])
    for pair in range(N_HEADS // 2):
        qr = qr4[:, pair * LANE:(pair + 1) * LANE]
        q2 = qr * qr
        ms_lo = jnp.sum(jnp.where(low_half, q2, 0.0), axis=-1, keepdims=True)
        ms_hi = jnp.sum(jnp.where(low_half, 0.0, q2), axis=-1, keepdims=True)
        ms = jnp.where(low_half, ms_lo, ms_hi) * (1.0 / MLA_ROPE)
        qr = _rope(qr * lax.rsqrt(ms + EPS) * gqr_ref[...], cos, sin) * sm_scale
        h0 = 2 * pair
        mq_ref[:, h0 * MLA_SLOT + MLA_NOPE:(h0 + 1) * MLA_SLOT] = jnp.where(low_half, qr, 0.0).astype(BF16)
        mq_ref[:, (h0 + 1) * MLA_SLOT + MLA_NOPE:(h0 + 2) * MLA_SLOT] = jnp.where(
            low_half, pltpu.roll(qr, HEAD_DIM, 1), 0.0).astype(BF16)


def _reorder_w_in(w_in):
    d = w_in.shape[0]
    o = np.cumsum([0, GROUP, GROUP, GROUP, GROUP, MLA_Q_RANK, MLA_KV_RANK, MLA_ROPE,
                   GROUP, GROUP, GROUP, GLA_GATE_RANK, GROUP])
    seg = [w_in[:, o[i]:o[i + 1]] for i in range(12)]
    ret_q, ret_k, ret_v, ret_g, cq, ckv, kr, gq, gk, gv, a_low, gr = seg
    pad = jnp.zeros((d, LANE - MLA_ROPE - GLA_GATE_RANK), w_in.dtype)
    return jnp.concatenate([ret_q, ret_k, ret_v, ret_g, cq, kr, a_low, pad, ckv, gq, gk, gv, gr], axis=1)


def _inproj(x, cos, sin, norm, w_in, mla_q_norm, mla_w_uq, mla_kv_norm, mla_w_ukv,
            q_nope_norm, q_rope_norm, k_nope_norm, k_rope_norm, gla_w_gate_up, gla_gate_bias, batch, tm):
    t, d = x.shape
    seq = t // batch
    tiles_per_seq = seq // tm
    w = _reorder_w_in(w_in).astype(BF16)
    qd = MLA_NOPE + MLA_ROPE
    wuq = mla_w_uq.reshape(MLA_Q_RANK, N_HEADS, qd)
    wuq = jnp.concatenate([wuq[:, :, :MLA_NOPE].reshape(MLA_Q_RANK, -1),
                           wuq[:, :, MLA_NOPE:].reshape(MLA_Q_RANK, -1)], axis=1).astype(BF16)
    wukv = mla_w_ukv.reshape(MLA_KV_RANK, N_HEADS, MLA_NOPE + MLA_DV)
    wukv = jnp.concatenate([wukv[:, :, :MLA_NOPE].reshape(MLA_KV_RANK, -1),
                            wukv[:, :, MLA_NOPE:].reshape(MLA_KV_RANK, -1)], axis=1).astype(BF16)
    gqr = jnp.tile(q_rope_norm, LANE // MLA_ROPE).reshape(1, LANE)
    gkr = jnp.concatenate([k_rope_norm, jnp.zeros(LANE - MLA_ROPE, F32)]).reshape(1, LANE)
    wga = jnp.zeros((LANE, GROUP), F32).at[MLA_ROPE:MLA_ROPE + GLA_GATE_RANK].set(gla_w_gate_up).astype(BF16)

    row = lambda n: pl.BlockSpec((tm, n), lambda i: (i, 0))
    out = lambda n, dt: jax.ShapeDtypeStruct((t, n), dt)
    wide = N_HEADS * MLA_SLOT
    keys_t = pl.BlockSpec((None, N_HEADS, MLA_SLOT, tm), lambda i: (i // tiles_per_seq, 0, 0, i % tiles_per_seq))
    return pl.pallas_call(
        _inproj_body,
        grid=(t // tm,),
        in_specs=[row(d), _const_spec((1, d)), row(LANE), row(LANE), _const_spec((d, W_COLS)),
                  _const_spec((1, MLA_Q_RANK)), _const_spec(wuq.shape), _const_spec((1, MLA_KV_RANK)),
                  _const_spec(wukv.shape), _const_spec((1, MLA_NOPE)), _const_spec((1, LANE)),
                  _const_spec((1, MLA_NOPE)), _const_spec((1, LANE)), _const_spec((LANE, GROUP)),
                  _const_spec((1, GROUP))],
        out_specs=[row(GROUP)] * 4 + [row(wide), keys_t, row(wide)] + [row(GROUP)] * 5,
        out_shape=[out(GROUP, BF16)] * 4
                  + [out(wide, BF16), jax.ShapeDtypeStruct((batch, N_HEADS, MLA_SLOT, seq), BF16), out(wide, BF16)]
                  + [out(GROUP, F32), out(GROUP, F32), out(GROUP, BF16), out(GROUP, F32), out(GROUP, BF16)],
        compiler_params=_params("parallel"),
        name="inproj",
    )(x, norm.reshape(1, d), cos, sin, w, mla_q_norm.reshape(1, -1), wuq, mla_kv_norm.reshape(1, -1), wukv,
      q_nope_norm.reshape(1, -1), gqr, k_nope_norm.reshape(1, -1), gkr, wga, gla_gate_bias.reshape(1, -1))


def _head_masks(shape):
    lane = lax.broadcasted_iota(jnp.int32, shape, 1)
    return [(lane >> int(math.log2(HEAD_DIM))) == h for h in range(N_HEADS)]


def _stack_heads(x, stack_mask):
    return jnp.concatenate([x] * N_HEADS, axis=0) * stack_mask


def _unstack_heads(y, masks, c):
    out = y[(N_HEADS - 1) * c:N_HEADS * c]
    for h in range(N_HEADS - 2, -1, -1):
        out = jnp.where(masks[h], y[h * c:(h + 1) * c], out)
    return out


def _stack_mask_const():
    head = np.arange(GROUP) // HEAD_DIM
    block = np.repeat(np.arange(N_HEADS), CHUNK)
    return jnp.asarray((block[:, None] == head[None, :]).astype(np.float32), BF16)


def _head_norm_gate(o, gate, gmat, norm):
    ms = _dot((o * o).astype(BF16), gmat) * (1.0 / HEAD_DIM)
    return (gate.astype(F32) * (o * lax.rsqrt(ms + EPS) * norm)).astype(BF16)


def _block_diag(dtype):
    idx = np.arange(GROUP) // HEAD_DIM
    return jnp.asarray((idx[:, None] == idx[None, :]).astype(np.float32), dtype)


def _ret_body(q_ref, k_ref, v_ref, g_ref, intra_ref, xi_ref, zeta_ref, cd_ref, gmat_ref, smask_ref, norm_ref,
              o_ref, state_ref, *, n_chunks):
    @pl.when(pl.program_id(1) == 0)
    def _():
        state_ref[...] = jnp.zeros_like(state_ref)

    masks = _head_masks((CHUNK, GROUP))
    for c in range(n_chunks):
        rows = slice(c * CHUNK, (c + 1) * CHUNK)
        q = q_ref[rows, :]
        k = k_ref[rows, :]
        v = v_ref[rows, :]
        sc = lax.dot_general(_stack_heads(q, smask_ref[...]), k, NT_DIMS, preferred_element_type=F32) * intra_ref[...]
        o = _unstack_heads(_dot(sc.astype(BF16), v), masks, CHUNK)
        state = state_ref[...]
        o = o + _dot(q, state.astype(BF16)) * xi_ref[...]
        vz = (v.astype(F32) * zeta_ref[...]).astype(BF16)
        kv = lax.dot_general(k, vz, TN_DIMS, preferred_element_type=F32)
        state_ref[...] = state * cd_ref[...] + kv * (cd_ref[...] > 0.0).astype(F32)
        o_ref[rows, :] = _head_norm_gate(o, g_ref[rows, :], gmat_ref[...], norm_ref[...])


def _retention(q, k, v, gate, norm, batch, rows_per_step):
    t = q.shape[0]
    seq = t // batch
    steps = seq // rows_per_step
    log_g = np.log1p(-np.power(2.0, -5.0 - np.arange(N_HEADS, dtype=np.float64)))
    idx = np.arange(CHUNK, dtype=np.float64)
    diff = idx[:, None] - idx[None, :]
    intra = np.where(diff >= 0, np.exp(log_g[:, None, None] * np.maximum(diff, 0.0)), 0.0)
    intra = intra.reshape(N_HEADS * CHUNK, CHUNK)
    lane_g = np.repeat(log_g, HEAD_DIM)[None, :]
    xi = np.exp(lane_g * (idx[:, None] + 1.0))
    zeta = np.exp(lane_g * (CHUNK - 1.0 - idx[:, None]))
    head = np.arange(GROUP) // HEAD_DIM
    cd = (head[:, None] == head[None, :]) * np.exp(lane_g * CHUNK)
    consts = [jnp.asarray(a, F32) for a in (intra, xi, zeta, cd)]
    row = pl.BlockSpec((rows_per_step, GROUP), lambda b, i: (b * steps + i, 0))
    return pl.pallas_call(
        functools.partial(_ret_body, n_chunks=rows_per_step // CHUNK),
        grid=(batch, steps),
        in_specs=[row] * 4 + [_const_spec(c.shape) for c in consts]
                 + [_const_spec((GROUP, GROUP)), _const_spec((N_HEADS * CHUNK, GROUP)), _const_spec((1, GROUP))],
        out_specs=row,
        out_shape=jax.ShapeDtypeStruct((t, GROUP), BF16),
        scratch_shapes=[pltpu.VMEM((GROUP, GROUP), F32)],
        compiler_params=_params("parallel", "arbitrary"),
        name="retention",
    )(q, k, v, gate, *consts, _block_diag(BF16), _stack_mask_const(), norm.reshape(1, GROUP))


SUBLANES = 8


def _boundary_rows(b, m, row):
    if 2 * m >= SUBLANES:
        pieces = []
        for j in range(CHUNK // (2 * m)):
            r = (2 * j + 1) * m - 1
            pieces.append(jnp.broadcast_to(b[r:r + 1, :], (2 * m, GROUP)))
        return jnp.concatenate(pieces, axis=0)
    tiles = b.reshape(CHUNK // SUBLANES, SUBLANES, GROUP)
    sub = (row & (SUBLANES - 1)).reshape(tiles.shape)
    out = None
    for j in range(SUBLANES // (2 * m) - 1, -1, -1):
        r = (2 * j + 1) * m - 1
        piece = jnp.broadcast_to(tiles[:, r:r + 1, :], tiles.shape)
        out = piece if out is None else jnp.where(sub < (2 * j + 2) * m, piece, out)
    return out.reshape(CHUNK, GROUP)


def _gla_body(q_ref, k_ref, v_ref, a_ref, g_ref, ltri_ref, lmask_ref, gmat_ref, bd_ref, smask_ref, norm_ref,
              o_ref, state_ref, *, n_chunks):
    @pl.when(pl.program_id(1) == 0)
    def _():
        state_ref[...] = jnp.zeros_like(state_ref)

    masks = _head_masks((CHUNK, GROUP))
    row = lax.broadcasted_iota(jnp.int32, (CHUNK, GROUP), 0)
    gmat = gmat_ref[...]
    ltri = ltri_ref[...]
    for c in range(n_chunks):
        rows = slice(c * CHUNK, (c + 1) * CHUNK)
        q = q_ref[rows, :]
        k = k_ref[rows, :]
        vb = v_ref[rows, :]
        v = vb.astype(F32)
        a = a_ref[rows, :]
        a0 = a.astype(BF16)
        r1 = a - a0.astype(F32)
        a1 = r1.astype(BF16)
        a2 = (r1 - a1.astype(F32)).astype(BF16)
        b = _dot(ltri, a0) + _dot(ltri, a1) + _dot(ltri, a2)
        b_last = b[CHUNK - 1:CHUNK, :]

        state = state_ref[...]
        o = lax.dot_general((q * jnp.exp(b)).astype(BF16), state.astype(BF16), NT_DIMS, preferred_element_type=F32)
        kd = (k * jnp.exp(b_last - b)).astype(BF16)
        kv = lax.dot_general(vb, kd, TN_DIMS, preferred_element_type=F32)
        state_ref[...] = state * jnp.exp(b_last) + kv * bd_ref[...]

        sc = None
        for li, m in enumerate(GLA_LEVELS):
            bref = _boundary_rows(b, m, row)
            upper = ((row >> int(math.log2(m))) & 1) == 1
            w = jnp.exp(jnp.where(upper, b - bref, bref - b))
            qt = jnp.where(upper, q * w, 0.0).astype(BF16)
            kt = jnp.where(upper, 0.0, k * w).astype(BF16)
            term = lax.dot_general(_stack_heads(qt, smask_ref[...]), kt, NT_DIMS, preferred_element_type=F32) * lmask_ref[li]
            sc = term if sc is None else sc + term
        o = o + _unstack_heads(_dot(sc.astype(BF16), vb), masks, CHUNK)

        qk = q * k
        qk_hi = qk.astype(BF16)
        prods = [qk_hi, (qk - qk_hi.astype(F32)).astype(BF16)]
        for d in range(1, LEAF):
            valid = (row & (LEAF - 1)) >= d
            w = jnp.exp(jnp.where(valid, b - pltpu.roll(b, d, 0), NEG))
            prods.append((q * pltpu.roll(k, d, 0) * w).astype(BF16))
        sums = _dot(jnp.concatenate(prods, axis=0), gmat)
        o = o + (sums[0:CHUNK] + sums[CHUNK:2 * CHUNK]) * v
        for d in range(1, LEAF):
            o = o + sums[(d + 1) * CHUNK:(d + 2) * CHUNK] * pltpu.roll(v, d, 0)

        o_ref[rows, :] = _head_norm_gate(o, g_ref[rows, :], gmat, norm_ref[...])


def _gla(q, k, v, log_a, gate, norm, batch, rows_per_step):
    t = q.shape[0]
    seq = t // batch
    steps = seq // rows_per_step
    idx = np.arange(CHUNK)
    ltri = jnp.asarray(idx[:, None] >= idx[None, :], BF16)
    lm = []
    for m in GLA_LEVELS:
        same = (idx[:, None] // (2 * m)) == (idx[None, :] // (2 * m))
        ok = same & (((idx[:, None] // m) & 1) == 1) & (((idx[None, :] // m) & 1) == 0)
        lm.append(np.tile(ok, (N_HEADS, 1)))
    lmask = jnp.asarray(np.stack(lm), F32)
    row = pl.BlockSpec((rows_per_step, GROUP), lambda b, i: (b * steps + i, 0))
    return pl.pallas_call(
        functools.partial(_gla_body, n_chunks=rows_per_step // CHUNK),
        grid=(batch, steps),
        in_specs=[row] * 5 + [_const_spec((CHUNK, CHUNK)), _const_spec(lmask.shape), _const_spec((GROUP, GROUP)),
                              _const_spec((GROUP, GROUP)), _const_spec((N_HEADS * CHUNK, GROUP)), _const_spec((1, GROUP))],
        out_specs=row,
        out_shape=jax.ShapeDtypeStruct((t, GROUP), BF16),
        scratch_shapes=[pltpu.VMEM((GROUP, GROUP), F32)],
        compiler_params=_params("parallel", "arbitrary"),
        name="gla",
    )(q, k, v, log_a, gate, ltri, lmask, _block_diag(BF16), _block_diag(F32), _stack_mask_const(), norm.reshape(1, GROUP))


def _mla_body(q_ref, k_ref, v_ref, o_ref, s0_ref, s1_ref, bm0_ref, bm1_ref, m_ref, acc_ref, *, tq, tk):
    i = pl.program_id(2)
    r = tq // tk
    q = q_ref[...]
    slots = ((s0_ref, bm0_ref), (s1_ref, bm1_ref))

    def scores(blk):
        start = pl.multiple_of(blk * tk, tk)
        return _dot(q, k_ref[:, pl.ds(start, tk)])

    def stash(slot, s):
        s_ref, bm_ref = slots[slot]
        s_ref[...] = s
        bm_ref[...] = jnp.broadcast_to(jnp.max(s, axis=-1, keepdims=True), bm_ref.shape)

    def absorb(slot, blk):
        s_ref, bm_ref = slots[slot]
        start = pl.multiple_of(blk * tk, tk)
        m_prev = m_ref[...]
        m_new = jnp.maximum(m_prev, bm_ref[...])
        m_wide = jnp.concatenate([m_new] * (tk // LANE), axis=1)
        p = jnp.exp2(s_ref[...] - m_wide).astype(BF16)
        alpha = jnp.exp2(m_prev - m_new)
        alpha_wide = jnp.concatenate([alpha] * (MLA_SLOT // LANE), axis=1)
        acc_ref[...] = acc_ref[...] * alpha_wide + _dot(p, v_ref[pl.ds(start, tk), :])
        m_ref[...] = m_new

    def block_at(n):
        return jnp.where(n < r, i * r + n, i * r + (r - 1) - n)

    m_ref[...] = jnp.full(m_ref.shape, NEG, F32)
    acc_ref[...] = jnp.zeros_like(acc_ref)
    row = lax.broadcasted_iota(jnp.int32, (tq, tk), 0)
    col = lax.broadcasted_iota(jnp.int32, (tq, tk), 1)
    for n in range(r):
        stash(n & 1, jnp.where(col + n * tk <= row, scores(i * r + n), NEG))
        if n > 0:
            absorb((n - 1) & 1, i * r + n - 1)

    def body(n, carry):
        for slot in (0, 1):
            @pl.when((n & 1) == slot)
            def _():
                stash(slot, scores(block_at(n)))
                absorb(1 - slot, block_at(n - 1))
        return carry

    steps = (i + 1) * r
    lax.fori_loop(r, steps, body, 0)
    for slot in (0, 1):
        @pl.when(((steps - 1) & 1) == slot)
        def _():
            absorb(slot, block_at(steps - 1))
    acc = acc_ref[...]
    o_ref[...] = (acc[:, :MLA_DV] / acc[:, MLA_DV:MLA_DV + 1]).astype(BF16)


def _mla_attention(q, k, v, batch, tq, tk):
    t = q.shape[0]
    seq = t // batch
    nq = seq // tq
    q3, v3 = (a.reshape(batch, seq, N_HEADS * MLA_SLOT) for a in (q, v))
    k_spec = pl.BlockSpec((None, None, MLA_SLOT, seq), lambda b, h, i: (b, h, 0, 0))
    v_spec = pl.BlockSpec((None, seq, MLA_SLOT), lambda b, h, i: (b, 0, h))
    out = pl.pallas_call(
        functools.partial(_mla_body, tq=tq, tk=tk),
        grid=(batch, N_HEADS, nq),
        in_specs=[pl.BlockSpec((None, tq, MLA_SLOT), lambda b, h, i: (b, i, h)), k_spec, v_spec],
        out_specs=pl.BlockSpec((None, tq, MLA_DV), lambda b, h, i: (b, i, h)),
        out_shape=jax.ShapeDtypeStruct((batch, seq, N_HEADS * MLA_DV), BF16),
        scratch_shapes=[pltpu.VMEM((tq, tk), F32), pltpu.VMEM((tq, tk), F32), pltpu.VMEM((tq, LANE), F32),
                        pltpu.VMEM((tq, LANE), F32), pltpu.VMEM((tq, LANE), F32), pltpu.VMEM((tq, MLA_SLOT), F32)],
        compiler_params=_params("parallel", "parallel", "arbitrary"),
        name="mla_attention",
    )(q3, k, v3)
    return out.reshape(t, N_HEADS * MLA_DV)


def _tiles(batch, seq):
    tm = min(512, seq)
    rows_per_step = min(1024, seq)
    tk = min(1024, seq)
    tq = tk
    return tm, rows_per_step, tq, tk


def kernel(x, positions, ffn1_norm, ffn1_w_gate_up, ffn1_w_down, mix_norm, w_in, ret_out_norm, mla_q_norm, mla_w_uq, mla_kv_norm, mla_w_ukv, mla_q_nope_norm, mla_q_rope_norm, mla_k_nope_norm, mla_k_rope_norm, gla_w_gate_up, gla_gate_bias, gla_out_norm, w_out, ffn2_norm, ffn2_w_gate_up, ffn2_w_down):
    batch, seq, d_model = x.shape
    depth = w_in.shape[0]
    tm, rows_per_step, tq, tk = _tiles(batch, seq)
    xt = x.reshape(batch * seq, d_model)
    cos, sin = _rope_tables(positions, tm)
    for l in range(depth):
        xt = _ffn(xt, ffn1_norm[l], ffn1_w_gate_up[l], ffn1_w_down[l], tm)
        (rq, rk, rv, rg, mq, mk, mv, gq, gk, gv, la, gg) = _inproj(
            xt, cos, sin, mix_norm[l], w_in[l], mla_q_norm[l], mla_w_uq[l], mla_kv_norm[l], mla_w_ukv[l],
            mla_q_nope_norm[l], mla_q_rope_norm[l], mla_k_nope_norm[l], mla_k_rope_norm[l],
            gla_w_gate_up[l], gla_gate_bias[l], batch, tm)
        ret = _retention(rq, rk, rv, rg, ret_out_norm[l], batch, rows_per_step)
        mla = _mla_attention(mq, mk, mv, batch, tq, tk)
        gla = _gla(gq, gk, gv, la, gg, gla_out_norm[l], batch, rows_per_step)
        xt = _ffn(xt, ffn2_norm[l], ffn2_w_gate_up[l], ffn2_w_down[l], tm, mix=(ret, mla, gla, w_out[l]))
    return xt.reshape(batch, seq, d_model)
```

```python
import functools
import math

import numpy as np
import jax
import jax.numpy as jnp
from jax import lax
from jax.experimental import pallas as pl
from jax.experimental.pallas import tpu as pltpu

F32 = jnp.float32
BF16 = jnp.bfloat16

EPS = 1e-6
ROPE_THETA = 10000.0
HEAD_DIM = 64
N_HEADS = 4
GROUP = N_HEADS * HEAD_DIM
MLA_NOPE = 128
MLA_ROPE = 64
MLA_DV = 128
MLA_Q_RANK = 384
MLA_KV_RANK = 256
MLA_SLOT = 256
GLA_GATE_RANK = 16
GLA_TAU = 16.0
CHUNK = 128
LEAF = 1
GLA_LEVELS = (64, 32, 16, 8, 4, 2, 1)
NEG = -1e30

LANE = 128
MXU_N = 256
VMEM_LIMIT_BYTES = 56 * 1024 * 1024

NT_DIMS = (((1,), (1,)), ((), ()))
TN_DIMS = (((0,), (0,)), ((), ()))


def _params(*sem):
    return pltpu.CompilerParams(dimension_semantics=sem, vmem_limit_bytes=VMEM_LIMIT_BYTES)


def _const_spec(shape):
    nd = len(shape)
    return pl.BlockSpec(shape, lambda *_: (0,) * nd, pipeline_mode=pl.Buffered(1))


def _dot(a, b):
    return jnp.dot(a, b, preferred_element_type=F32)


def _rms(x, g):
    return x * lax.rsqrt(jnp.mean(x * x, axis=-1, keepdims=True) + EPS) * g


def _silu(x):
    return x * jax.nn.sigmoid(x)


def _rope_table_body(pos_ref, inv_ref, sign_ref, cos_ref, sin_ref):
    ang = pos_ref[...].astype(F32) * inv_ref[...]
    cos_ref[...] = jnp.cos(ang)
    sin_ref[...] = jnp.sin(ang) * sign_ref[...]


def _rope_tables(positions, tm):
    t = positions.size
    half = HEAD_DIM // 2
    inv = ROPE_THETA ** (-jnp.arange(0, HEAD_DIM, 2, dtype=F32) / HEAD_DIM)
    inv_row = jnp.tile(inv, LANE // half).reshape(1, LANE)
    sign_row = jnp.tile(jnp.concatenate([-jnp.ones(half, F32), jnp.ones(half, F32)]), LANE // HEAD_DIM).reshape(1, LANE)
    pos = positions.reshape(t, 1)
    row = pl.BlockSpec((tm, LANE), lambda i: (i, 0))
    return pl.pallas_call(
        _rope_table_body,
        grid=(t // tm,),
        in_specs=[pl.BlockSpec((tm, 1), lambda i: (i, 0)), _const_spec((1, LANE)), _const_spec((1, LANE))],
        out_specs=[row, row],
        out_shape=[jax.ShapeDtypeStruct((t, LANE), F32)] * 2,
        compiler_params=_params("parallel"),
        name="rope_tables",
    )(pos, inv_row, sign_row)


def _rope(x, cos, sin):
    w = x.shape[-1]
    reps = w // LANE
    if reps > 1:
        cos = jnp.concatenate([cos] * reps, axis=-1)
        sin = jnp.concatenate([sin] * reps, axis=-1)
    lane = lax.broadcasted_iota(jnp.int32, x.shape, 1)
    first_half = (lane & (HEAD_DIM // 2)) == 0
    swapped = jnp.where(first_half, pltpu.roll(x, w - HEAD_DIM // 2, 1), pltpu.roll(x, HEAD_DIM // 2, 1))
    return x * cos + swapped * sin


def _ffn_body(*refs, d_ff, d_model, with_mix):
    if with_mix:
        x_ref, r_ref, m_ref, a_ref, wr_ref, wm_ref, wa_ref, g_ref, wgu_ref, wd_ref, o_ref, act_ref, x_mid_ref = refs
        r, m, a = r_ref[...], m_ref[...], a_ref[...]
        for c in range(d_model // MXU_N):
            cols = slice(c * MXU_N, (c + 1) * MXU_N)
            x_mid_ref[:, cols] = x_ref[:, cols] + (
                _dot(r, wr_ref[:, cols]) + _dot(m, wm_ref[:, cols]) + _dot(a, wa_ref[:, cols]))
        x_ref = x_mid_ref
    else:
        x_ref, g_ref, wgu_ref, wd_ref, o_ref, act_ref = refs
    xn = _rms(x_ref[...], g_ref[...]).astype(BF16)
    for c in range(d_ff // MXU_N):
        cols = slice(c * MXU_N, (c + 1) * MXU_N)
        ucols = slice(d_ff + c * MXU_N, d_ff + (c + 1) * MXU_N)
        g = _dot(xn, wgu_ref[:, cols])
        u = _dot(xn, wgu_ref[:, ucols])
        act_ref[:, cols] = (_silu(g) * u).astype(BF16)
    act = act_ref[...]
    for c in range(d_model // MXU_N):
        cols = slice(c * MXU_N, (c + 1) * MXU_N)
        o_ref[:, cols] = x_ref[:, cols] + 0.5 * _dot(act, wd_ref[:, cols])


def _ffn(x, norm, w_gate_up, w_down, tm, mix=None):
    t, d_model = x.shape
    d_ff = w_down.shape[0]
    row = lambda n: pl.BlockSpec((tm, n), lambda i: (i, 0))
    operands, specs, scratch = [x], [row(d_model)], [pltpu.VMEM((tm, d_ff), BF16)]
    if mix is not None:
        ret, mla, gla, w_out = mix
        nr, nm, na = ret.shape[1], mla.shape[1], gla.shape[1]
        w = w_out.astype(BF16)
        operands += [ret, mla, gla, w[:nr], w[nr:nr + nm], w[nr + nm:]]
        specs += [row(nr), row(nm), row(na), _const_spec((nr, d_model)), _const_spec((nm, d_model)),
                  _const_spec((na, d_model))]
        scratch.append(pltpu.VMEM((tm, d_model), F32))
    operands += [norm.reshape(1, d_model), w_gate_up.astype(BF16), w_down.astype(BF16)]
    specs += [_const_spec((1, d_model)), _const_spec((d_model, 2 * d_ff)), _const_spec((d_ff, d_model))]
    return pl.pallas_call(
        functools.partial(_ffn_body, d_ff=d_ff, d_model=d_model, with_mix=mix is not None),
        grid=(t // tm,),
        in_specs=specs,
        out_specs=row(d_model),
        out_shape=jax.ShapeDtypeStruct((t, d_model), F32),
        scratch_shapes=scratch,
        compiler_params=_params("parallel"),
        name="mix_ffn" if mix is not None else "ffn",
    )(*operands)


W_RET = 0
W_CQ = 4 * GROUP
W_TAIL = W_CQ + MLA_Q_RANK
W_CKV = W_TAIL + LANE
W_GLA = W_CKV + MLA_KV_RANK
W_GLA_R = W_GLA + 3 * GROUP
W_COLS = W_GLA_R + GROUP


def _inproj_body(x_ref, g_ref, cos_ref, sin_ref, w_ref, qn_ref, wuq_ref, kvn_ref, wukv_ref,
                 gqn_ref, gqr_ref, gkn_ref, gkr_ref, wga_ref, ba_ref,
                 rq_ref, rk_ref, rv_ref, rg_ref, mq_ref, mk_ref, mv_ref,
                 gq_ref, gk_ref, gv_ref, la_ref, gg_ref):
    cos = cos_ref[...]
    sin = sin_ref[...]
    xn = _rms(x_ref[...], g_ref[...]).astype(BF16)
    tm = xn.shape[0]
    lane = lax.broadcasted_iota(jnp.int32, (tm, LANE), 1)
    low_half = lane < HEAD_DIM
    dk_scale = HEAD_DIM ** -0.5

    def proj(lo, width):
        return _dot(xn, w_ref[:, lo:lo + width])

    rq_ref[...] = _rope(proj(W_RET, GROUP), cos, sin).astype(BF16)
    rk_ref[...] = (_rope(proj(W_RET + GROUP, GROUP), cos, sin) * dk_scale).astype(BF16)
    rv_ref[...] = proj(W_RET + 2 * GROUP, GROUP).astype(BF16)
    rg_ref[...] = _silu(proj(W_RET + 3 * GROUP, GROUP)).astype(BF16)

    gq_ref[...] = proj(W_GLA, GROUP) * dk_scale
    gk_ref[...] = proj(W_GLA + GROUP, GROUP)
    gv_ref[...] = proj(W_GLA + 2 * GROUP, GROUP).astype(BF16)
    gg_ref[...] = _silu(proj(W_GLA_R, GROUP)).astype(BF16)
    cq_tail = proj(W_CQ, MLA_Q_RANK + LANE)
    tail = cq_tail[:, MLA_Q_RANK:]
    z = _dot(tail.astype(BF16), wga_ref[...]) + ba_ref[...]
    la_ref[...] = (jnp.minimum(z, 0.0) - jnp.log(1.0 + jnp.exp(-jnp.abs(z)))) * (1.0 / GLA_TAU)

    kr_ms = jnp.sum(jnp.where(low_half, tail * tail, 0.0), axis=-1, keepdims=True) * (1.0 / MLA_ROPE)
    kr = _rope(tail * lax.rsqrt(kr_ms + EPS) * gkr_ref[...], cos, sin)
    kr_t = kr.T.astype(BF16)
    ckv = _rms(proj(W_CKV, MLA_KV_RANK), kvn_ref[...]).astype(BF16)
    ones_col = jnp.where(lane == 0, 1.0, 0.0).astype(BF16)
    v_base = N_HEADS * MLA_NOPE
    for pair in range(N_HEADS // 2):
        kn2 = _dot(ckv, wukv_ref[:, pair * MXU_N:(pair + 1) * MXU_N])
        v2 = _dot(ckv, wukv_ref[:, v_base + pair * MXU_N:v_base + (pair + 1) * MXU_N])
        for sub in range(2):
            h = 2 * pair + sub
            kn = kn2[:, sub * MLA_NOPE:(sub + 1) * MLA_NOPE]
            mk_ref[h, 0:MLA_NOPE, :] = _rms(kn, gkn_ref[...]).T.astype(BF16)
            mk_ref[h, MLA_NOPE:MLA_SLOT, :] = kr_t
            mv_ref[:, h * MLA_SLOT:h * MLA_SLOT + MLA_DV] = v2[:, sub * MLA_DV:(sub + 1) * MLA_DV].astype(BF16)
            mv_ref[:, h * MLA_SLOT + MLA_DV:(h + 1) * MLA_SLOT] = ones_col

    sm_scale = (MLA_NOPE + MLA_ROPE) ** -0.5 * math.log2(math.e)
    cq = _rms(cq_tail[:, :MLA_Q_RANK], qn_ref[...]).astype(BF16)
    for pair in range(N_HEADS // 2):
        qn2 = _dot(cq, wuq_ref[:, pair * MXU_N:(pair + 1) * MXU_N])
        for sub in range(2):
            h = 2 * pair + sub
            qn = qn2[:, sub * MLA_NOPE:(sub + 1) * MLA_NOPE]
            mq_ref[:, h * MLA_SLOT:h * MLA_SLOT + MLA_NOPE] = (_rms(qn, gqn_ref[...]) * sm_scale).astype(BF16)
    qr4 = _dot(cq, wuq_ref[:, N_HEADS * MLA_NOPE:N_HEADS * MLA_NOPE + N_HEADS * MLA_ROPE])
    for pair in range(N_HEADS // 2):
        qr = qr4[:, pair * LANE:(pair + 1) * LANE]
        q2 = qr * qr
        ms_lo = jnp.sum(jnp.where(low_half, q2, 0.0), axis=-1, keepdims=True)
        ms_hi = jnp.sum(jnp.where(low_half, 0.0, q2), axis=-1, keepdims=True)
        ms = jnp.where(low_half, ms_lo, ms_hi) * (1.0 / MLA_ROPE)
        qr = _rope(qr * lax.rsqrt(ms + EPS) * gqr_ref[...], cos, sin) * sm_scale
        h0 = 2 * pair
        mq_ref[:, h0 * MLA_SLOT + MLA_NOPE:(h0 + 1) * MLA_SLOT] = jnp.where(low_half, qr, 0.0).astype(BF16)
        mq_ref[:, (h0 + 1) * MLA_SLOT + MLA_NOPE:(h0 + 2) * MLA_SLOT] = jnp.where(
            low_half, pltpu.roll(qr, HEAD_DIM, 1), 0.0).astype(BF16)


def _reorder_w_in(w_in):
    d = w_in.shape[0]
    o = np.cumsum([0, GROUP, GROUP, GROUP, GROUP, MLA_Q_RANK, MLA_KV_RANK, MLA_ROPE,
                   GROUP, GROUP, GROUP, GLA_GATE_RANK, GROUP])
    seg = [w_in[:, o[i]:o[i + 1]] for i in range(12)]
    ret_q, ret_k, ret_v, ret_g, cq, ckv, kr, gq, gk, gv, a_low, gr = seg
    pad = jnp.zeros((d, LANE - MLA_ROPE - GLA_GATE_RANK), w_in.dtype)
    return jnp.concatenate([ret_q, ret_k, ret_v, ret_g, cq, kr, a_low, pad, ckv, gq, gk, gv, gr], axis=1)


def _inproj(x, cos, sin, norm, w_in, mla_q_norm, mla_w_uq, mla_kv_norm, mla_w_ukv,
            q_nope_norm, q_rope_norm, k_nope_norm, k_rope_norm, gla_w_gate_up, gla_gate_bias, batch, tm):
    t, d = x.shape
    seq = t // batch
    tiles_per_seq = seq // tm
    w = _reorder_w_in(w_in).astype(BF16)
    qd = MLA_NOPE + MLA_ROPE
    wuq = mla_w_uq.reshape(MLA_Q_RANK, N_HEADS, qd)
    wuq = jnp.concatenate([wuq[:, :, :MLA_NOPE].reshape(MLA_Q_RANK, -1),
                           wuq[:, :, MLA_NOPE:].reshape(MLA_Q_RANK, -1)], axis=1).astype(BF16)
    wukv = mla_w_ukv.reshape(MLA_KV_RANK, N_HEADS, MLA_NOPE + MLA_DV)
    wukv = jnp.concatenate([wukv[:, :, :MLA_NOPE].reshape(MLA_KV_RANK, -1),
                            wukv[:, :, MLA_NOPE:].reshape(MLA_KV_RANK, -1)], axis=1).astype(BF16)
    gqr = jnp.tile(q_rope_norm, LANE // MLA_ROPE).reshape(1, LANE)
    gkr = jnp.concatenate([k_rope_norm, jnp.zeros(LANE - MLA_ROPE, F32)]).reshape(1, LANE)
    wga = jnp.zeros((LANE, GROUP), F32).at[MLA_ROPE:MLA_ROPE + GLA_GATE_RANK].set(gla_w_gate_up).astype(BF16)

    row = lambda n: pl.BlockSpec((tm, n), lambda i: (i, 0))
    out = lambda n, dt: jax.ShapeDtypeStruct((t, n), dt)
    wide = N_HEADS * MLA_SLOT
    keys_t = pl.BlockSpec((None, N_HEADS, MLA_SLOT, tm), lambda i: (i // tiles_per_seq, 0, 0, i % tiles_per_seq))
    return pl.pallas_call(
        _inproj_body,
        grid=(t // tm,),
        in_specs=[row(d), _const_spec((1, d)), row(LANE), row(LANE), _const_spec((d, W_COLS)),
                  _const_spec((1, MLA_Q_RANK)), _const_spec(wuq.shape), _const_spec((1, MLA_KV_RANK)),
                  _const_spec(wukv.shape), _const_spec((1, MLA_NOPE)), _const_spec((1, LANE)),
                  _const_spec((1, MLA_NOPE)), _const_spec((1, LANE)), _const_spec((LANE, GROUP)),
                  _const_spec((1, GROUP))],
        out_specs=[row(GROUP)] * 4 + [row(wide), keys_t, row(wide)] + [row(GROUP)] * 5,
        out_shape=[out(GROUP, BF16)] * 4
                  + [out(wide, BF16), jax.ShapeDtypeStruct((batch, N_HEADS, MLA_SLOT, seq), BF16), out(wide, BF16)]
                  + [out(GROUP, F32), out(GROUP, F32), out(GROUP, BF16), out(GROUP, F32), out(GROUP, BF16)],
        compiler_params=_params("parallel"),
        name="inproj",
    )(x, norm.reshape(1, d), cos, sin, w, mla_q_norm.reshape(1, -1), wuq, mla_kv_norm.reshape(1, -1), wukv,
      q_nope_norm.reshape(1, -1), gqr, k_nope_norm.reshape(1, -1), gkr, wga, gla_gate_bias.reshape(1, -1))


def _head_masks(shape):
    lane = lax.broadcasted_iota(jnp.int32, shape, 1)
    return [(lane >> int(math.log2(HEAD_DIM))) == h for h in range(N_HEADS)]


def _stack_heads(x, stack_mask):
    return jnp.concatenate([x] * N_HEADS, axis=0) * stack_mask


def _unstack_heads(y, masks, c):
    out = y[(N_HEADS - 1) * c:N_HEADS * c]
    for h in range(N_HEADS - 2, -1, -1):
        out = jnp.where(masks[h], y[h * c:(h + 1) * c], out)
    return out


def _stack_mask_const():
    head = np.arange(GROUP) // HEAD_DIM
    block = np.repeat(np.arange(N_HEADS), CHUNK)
    return jnp.asarray((block[:, None] == head[None, :]).astype(np.float32), BF16)


def _head_norm_gate(o, gate, gmat, norm):
    ms = _dot((o * o).astype(BF16), gmat) * (1.0 / HEAD_DIM)
    return (gate.astype(F32) * (o * lax.rsqrt(ms + EPS) * norm)).astype(BF16)


def _block_diag(dtype):
    idx = np.arange(GROUP) // HEAD_DIM
    return jnp.asarray((idx[:, None] == idx[None, :]).astype(np.float32), dtype)


def _ret_body(q_ref, k_ref, v_ref, g_ref, intra_ref, xi_ref, zeta_ref, cd_ref, gmat_ref, smask_ref, norm_ref,
              o_ref, state_ref, *, n_chunks):
    @pl.when(pl.program_id(1) == 0)
    def _():
        state_ref[...] = jnp.zeros_like(state_ref)

    masks = _head_masks((CHUNK, GROUP))
    for c in range(n_chunks):
        rows = slice(c * CHUNK, (c + 1) * CHUNK)
        q = q_ref[rows, :]
        k = k_ref[rows, :]
        v = v_ref[rows, :]
        sc = lax.dot_general(_stack_heads(q, smask_ref[...]), k, NT_DIMS, preferred_element_type=F32) * intra_ref[...]
        o = _unstack_heads(_dot(sc.astype(BF16), v), masks, CHUNK)
        state = state_ref[...]
        o = o + _dot(q, state.astype(BF16)) * xi_ref[...]
        vz = (v.astype(F32) * zeta_ref[...]).astype(BF16)
        kv = lax.dot_general(k, vz, TN_DIMS, preferred_element_type=F32)
        state_ref[...] = state * cd_ref[...] + kv * (cd_ref[...] > 0.0).astype(F32)
        o_ref[rows, :] = _head_norm_gate(o, g_ref[rows, :], gmat_ref[...], norm_ref[...])


def _retention(q, k, v, gate, norm, batch, rows_per_step):
    t = q.shape[0]
    seq = t // batch
    steps = seq // rows_per_step
    log_g = np.log1p(-np.power(2.0, -5.0 - np.arange(N_HEADS, dtype=np.float64)))
    idx = np.arange(CHUNK, dtype=np.float64)
    diff = idx[:, None] - idx[None, :]
    intra = np.where(diff >= 0, np.exp(log_g[:, None, None] * np.maximum(diff, 0.0)), 0.0)
    intra = intra.reshape(N_HEADS * CHUNK, CHUNK)
    lane_g = np.repeat(log_g, HEAD_DIM)[None, :]
    xi = np.exp(lane_g * (idx[:, None] + 1.0))
    zeta = np.exp(lane_g * (CHUNK - 1.0 - idx[:, None]))
    head = np.arange(GROUP) // HEAD_DIM
    cd = (head[:, None] == head[None, :]) * np.exp(lane_g * CHUNK)
    consts = [jnp.asarray(a, F32) for a in (intra, xi, zeta, cd)]
    row = pl.BlockSpec((rows_per_step, GROUP), lambda b, i: (b * steps + i, 0))
    return pl.pallas_call(
        functools.partial(_ret_body, n_chunks=rows_per_step // CHUNK),
        grid=(batch, steps),
        in_specs=[row] * 4 + [_const_spec(c.shape) for c in consts]
                 + [_const_spec((GROUP, GROUP)), _const_spec((N_HEADS * CHUNK, GROUP)), _const_spec((1, GROUP))],
        out_specs=row,
        out_shape=jax.ShapeDtypeStruct((t, GROUP), BF16),
        scratch_shapes=[pltpu.VMEM((GROUP, GROUP), F32)],
        compiler_params=_params("parallel", "arbitrary"),
        name="retention",
    )(q, k, v, gate, *consts, _block_diag(BF16), _stack_mask_const(), norm.reshape(1, GROUP))


SUBLANES = 8


def _boundary_rows(b, m, row):
    if 2 * m >= SUBLANES:
        pieces = []
        for j in range(CHUNK // (2 * m)):
            r = (2 * j + 1) * m - 1
            pieces.append(jnp.broadcast_to(b[r:r + 1, :], (2 * m, GROUP)))
        return jnp.concatenate(pieces, axis=0)
    tiles = b.reshape(CHUNK // SUBLANES, SUBLANES, GROUP)
    sub = (row & (SUBLANES - 1)).reshape(tiles.shape)
    out = None
    for j in range(SUBLANES // (2 * m) - 1, -1, -1):
        r = (2 * j + 1) * m - 1
        piece = jnp.broadcast_to(tiles[:, r:r + 1, :], tiles.shape)
        out = piece if out is None else jnp.where(sub < (2 * j + 2) * m, piece, out)
    return out.reshape(CHUNK, GROUP)


def _gla_body(q_ref, k_ref, v_ref, a_ref, g_ref, ltri_ref, lmask_ref, gmat_ref, bd_ref, smask_ref, norm_ref,
              o_ref, state_ref, *, n_chunks):
    @pl.when(pl.program_id(1) == 0)
    def _():
        state_ref[...] = jnp.zeros_like(state_ref)

    masks = _head_masks((CHUNK, GROUP))
    row = lax.broadcasted_iota(jnp.int32, (CHUNK, GROUP), 0)
    gmat = gmat_ref[...]
    ltri = ltri_ref[...]
    for c in range(n_chunks):
        rows = slice(c * CHUNK, (c + 1) * CHUNK)
        q = q_ref[rows, :]
        k = k_ref[rows, :]
        vb = v_ref[rows, :]
        v = vb.astype(F32)
        a = a_ref[rows, :]
        a0 = a.astype(BF16)
        r1 = a - a0.astype(F32)
        a1 = r1.astype(BF16)
        a2 = (r1 - a1.astype(F32)).astype(BF16)
        b = _dot(ltri, a0) + _dot(ltri, a1) + _dot(ltri, a2)
        b_last = b[CHUNK - 1:CHUNK, :]

        state = state_ref[...]
        o = lax.dot_general((q * jnp.exp(b)).astype(BF16), state.astype(BF16), NT_DIMS, preferred_element_type=F32)
        kd = (k * jnp.exp(b_last - b)).astype(BF16)
        kv = lax.dot_general(vb, kd, TN_DIMS, preferred_element_type=F32)
        state_ref[...] = state * jnp.exp(b_last) + kv * bd_ref[...]

        sc = None
        for li, m in enumerate(GLA_LEVELS):
            bref = _boundary_rows(b, m, row)
            upper = ((row >> int(math.log2(m))) & 1) == 1
            w = jnp.exp(jnp.where(upper, b - bref, bref - b))
            qt = jnp.where(upper, q * w, 0.0).astype(BF16)
            kt = jnp.where(upper, 0.0, k * w).astype(BF16)
            term = lax.dot_general(_stack_heads(qt, smask_ref[...]), kt, NT_DIMS, preferred_element_type=F32) * lmask_ref[li]
            sc = term if sc is None else sc + term
        o = o + _unstack_heads(_dot(sc.astype(BF16), vb), masks, CHUNK)

        qk = q * k
        qk_hi = qk.astype(BF16)
        prods = [qk_hi, (qk - qk_hi.astype(F32)).astype(BF16)]
        for d in range(1, LEAF):
            valid = (row & (LEAF - 1)) >= d
            w = jnp.exp(jnp.where(valid, b - pltpu.roll(b, d, 0), NEG))
            prods.append((q * pltpu.roll(k, d, 0) * w).astype(BF16))
        sums = _dot(jnp.concatenate(prods, axis=0), gmat)
        o = o + (sums[0:CHUNK] + sums[CHUNK:2 * CHUNK]) * v
        for d in range(1, LEAF):
            o = o + sums[(d + 1) * CHUNK:(d + 2) * CHUNK] * pltpu.roll(v, d, 0)

        o_ref[rows, :] = _head_norm_gate(o, g_ref[rows, :], gmat, norm_ref[...])


def _gla(q, k, v, log_a, gate, norm, batch, rows_per_step):
    t = q.shape[0]
    seq = t // batch
    steps = seq // rows_per_step
    idx = np.arange(CHUNK)
    ltri = jnp.asarray(idx[:, None] >= idx[None, :], BF16)
    lm = []
    for m in GLA_LEVELS:
        same = (idx[:, None] // (2 * m)) == (idx[None, :] // (2 * m))
        ok = same & (((idx[:, None] // m) & 1) == 1) & (((idx[None, :] // m) & 1) == 0)
        lm.append(np.tile(ok, (N_HEADS, 1)))
    lmask = jnp.asarray(np.stack(lm), F32)
    row = pl.BlockSpec((rows_per_step, GROUP), lambda b, i: (b * steps + i, 0))
    return pl.pallas_call(
        functools.partial(_gla_body, n_chunks=rows_per_step // CHUNK),
        grid=(batch, steps),
        in_specs=[row] * 5 + [_const_spec((CHUNK, CHUNK)), _const_spec(lmask.shape), _const_spec((GROUP, GROUP)),
                              _const_spec((GROUP, GROUP)), _const_spec((N_HEADS * CHUNK, GROUP)), _const_spec((1, GROUP))],
        out_specs=row,
        out_shape=jax.ShapeDtypeStruct((t, GROUP), BF16),
        scratch_shapes=[pltpu.VMEM((GROUP, GROUP), F32)],
        compiler_params=_params("parallel", "arbitrary"),
        name="gla",
    )(q, k, v, log_a, gate, ltri, lmask, _block_diag(BF16), _block_diag(F32), _stack_mask_const(), norm.reshape(1, GROUP))


def _mla_body(q_ref, k_ref, v_ref, o_ref, s0_ref, s1_ref, bm0_ref, bm1_ref, m_ref, acc_ref, *, tq, tk):
    i = pl.program_id(2)
    r = tq // tk
    q = q_ref[...]
    slots = ((s0_ref, bm0_ref), (s1_ref, bm1_ref))

    def scores(blk):
        start = pl.multiple_of(blk * tk, tk)
        return _dot(q, k_ref[:, pl.ds(start, tk)])

    def stash(slot, s):
        s_ref, bm_ref = slots[slot]
        s_ref[...] = s
        bm_ref[...] = jnp.broadcast_to(jnp.max(s, axis=-1, keepdims=True), bm_ref.shape)

    def absorb(slot, blk):
        s_ref, bm_ref = slots[slot]
        start = pl.multiple_of(blk * tk, tk)
        m_prev = m_ref[...]
        m_new = jnp.maximum(m_prev, bm_ref[...])
        m_wide = jnp.concatenate([m_new] * (tk // LANE), axis=1)
        p = jnp.exp2(s_ref[...] - m_wide).astype(BF16)
        alpha = jnp.exp2(m_prev - m_new)
        alpha_wide = jnp.concatenate([alpha] * (MLA_SLOT // LANE), axis=1)
        acc_ref[...] = acc_ref[...] * alpha_wide + _dot(p, v_ref[pl.ds(start, tk), :])
        m_ref[...] = m_new

    def block_at(n):
        return jnp.where(n < r, i * r + n, i * r + (r - 1) - n)

    m_ref[...] = jnp.full(m_ref.shape, NEG, F32)
    acc_ref[...] = jnp.zeros_like(acc_ref)
    row = lax.broadcasted_iota(jnp.int32, (tq, tk), 0)
    col = lax.broadcasted_iota(jnp.int32, (tq, tk), 1)
    for n in range(r):
        stash(n & 1, jnp.where(col + n * tk <= row, scores(i * r + n), NEG))
        if n > 0:
            absorb((n - 1) & 1, i * r + n - 1)

    def body(n, carry):
        for slot in (0, 1):
            @pl.when((n & 1) == slot)
            def _():
                stash(slot, scores(block_at(n)))
                absorb(1 - slot, block_at(n - 1))
        return carry

    steps = (i + 1) * r
    lax.fori_loop(r, steps, body, 0)
    for slot in (0, 1):
        @pl.when(((steps - 1) & 1) == slot)
        def _():
            absorb(slot, block_at(steps - 1))
    acc = acc_ref[...]
    o_ref[...] = (acc[:, :MLA_DV] / acc[:, MLA_DV:MLA_DV + 1]).astype(BF16)


def _mla_attention(q, k, v, batch, tq, tk):
    t = q.shape[0]
    seq = t // batch
    nq = seq // tq
    q3, v3 = (a.reshape(batch, seq, N_HEADS * MLA_SLOT) for a in (q, v))
    k_spec = pl.BlockSpec((None, None, MLA_SLOT, seq), lambda b, h, i: (b, h, 0, 0))
    v_spec = pl.BlockSpec((None, seq, MLA_SLOT), lambda b, h, i: (b, 0, h))
    out = pl.pallas_call(
        functools.partial(_mla_body, tq=tq, tk=tk),
        grid=(batch, N_HEADS, nq),
        in_specs=[pl.BlockSpec((None, tq, MLA_SLOT), lambda b, h, i: (b, i, h)), k_spec, v_spec],
        out_specs=pl.BlockSpec((None, tq, MLA_DV), lambda b, h, i: (b, i, h)),
        out_shape=jax.ShapeDtypeStruct((batch, seq, N_HEADS * MLA_DV), BF16),
        scratch_shapes=[pltpu.VMEM((tq, tk), F32), pltpu.VMEM((tq, tk), F32), pltpu.VMEM((tq, LANE), F32),
                        pltpu.VMEM((tq, LANE), F32), pltpu.VMEM((tq, LANE), F32), pltpu.VMEM((tq, MLA_SLOT), F32)],
        compiler_params=_params("parallel", "parallel", "arbitrary"),
        name="mla_attention",
    )(q3, k, v3)
    return out.reshape(t, N_HEADS * MLA_DV)


def _tiles(batch, seq):
    tm = min(512, seq)
    tm_ffn = min(1024, seq)
    rows_per_step = min(1024, seq)
    tk = min(1024, seq)
    tq = tk
    return tm, tm_ffn, rows_per_step, tq, tk


def kernel(x, positions, ffn1_norm, ffn1_w_gate_up, ffn1_w_down, mix_norm, w_in, ret_out_norm, mla_q_norm, mla_w_uq, mla_kv_norm, mla_w_ukv, mla_q_nope_norm, mla_q_rope_norm, mla_k_nope_norm, mla_k_rope_norm, gla_w_gate_up, gla_gate_bias, gla_out_norm, w_out, ffn2_norm, ffn2_w_gate_up, ffn2_w_down):
    batch, seq, d_model = x.shape
    depth = w_in.shape[0]
    tm, tm_ffn, rows_per_step, tq, tk = _tiles(batch, seq)
    xt = x.reshape(batch * seq, d_model)
    cos, sin = _rope_tables(positions, tm)
    for l in range(depth):
        xt = _ffn(xt, ffn1_norm[l], ffn1_w_gate_up[l], ffn1_w_down[l], tm_ffn)
        (rq, rk, rv, rg, mq, mk, mv, gq, gk, gv, la, gg) = _inproj(
            xt, cos, sin, mix_norm[l], w_in[l], mla_q_norm[l], mla_w_uq[l], mla_kv_norm[l], mla_w_ukv[l],
            mla_q_nope_norm[l], mla_q_rope_norm[l], mla_k_nope_norm[l], mla_k_rope_norm[l],
            gla_w_gate_up[l], gla_gate_bias[l], batch, tm)
        ret = _retention(rq, rk, rv, rg, ret_out_norm[l], batch, rows_per_step)
        mla = _mla_attention(mq, mk, mv, batch, tq, tk)
        gla = _gla(gq, gk, gv, la, gg, gla_out_norm[l], batch, rows_per_step)
        xt = _ffn(xt, ffn2_norm[l], ffn2_w_gate_up[l], ffn2_w_down[l], tm_ffn, mix=(ret, mla, gla, w_out[l]))
    return xt.reshape(batch, seq, d_model)
```

```python
import functools
import math

import numpy as np
import jax
import jax.numpy as jnp
from jax import lax
from jax.experimental import pallas as pl
from jax.experimental.pallas import tpu as pltpu

F32 = jnp.float32
BF16 = jnp.bfloat16

EPS = 1e-6
ROPE_THETA = 10000.0
HEAD_DIM = 64
N_HEADS = 4
GROUP = N_HEADS * HEAD_DIM
MLA_NOPE = 128
MLA_ROPE = 64
MLA_DV = 128
MLA_Q_RANK = 384
MLA_KV_RANK = 256
MLA_SLOT = 256
GLA_GATE_RANK = 16
GLA_TAU = 16.0
CHUNK = 128
RET_CHUNK = 256
LEAF = 1
GLA_LEVELS = (64, 32, 16, 8, 4, 2, 1)
NEG = -1e30

LANE = 128
MXU_N = 256
VMEM_LIMIT_BYTES = 56 * 1024 * 1024

NT_DIMS = (((1,), (1,)), ((), ()))
TN_DIMS = (((0,), (0,)), ((), ()))


def _params(*sem):
    return pltpu.CompilerParams(dimension_semantics=sem, vmem_limit_bytes=VMEM_LIMIT_BYTES)


def _const_spec(shape):
    nd = len(shape)
    return pl.BlockSpec(shape, lambda *_: (0,) * nd, pipeline_mode=pl.Buffered(1))


def _dot(a, b):
    return jnp.dot(a, b, preferred_element_type=F32)


def _rms(x, g):
    return x * lax.rsqrt(jnp.mean(x * x, axis=-1, keepdims=True) + EPS) * g


def _silu(x):
    return x * jax.nn.sigmoid(x)


def _rope_table_body(pos_ref, inv_ref, sign_ref, cos_ref, sin_ref):
    ang = pos_ref[...].astype(F32) * inv_ref[...]
    cos_ref[...] = jnp.cos(ang)
    sin_ref[...] = jnp.sin(ang) * sign_ref[...]


def _rope_tables(positions, tm):
    t = positions.size
    half = HEAD_DIM // 2
    inv = ROPE_THETA ** (-jnp.arange(0, HEAD_DIM, 2, dtype=F32) / HEAD_DIM)
    inv_row = jnp.tile(inv, LANE // half).reshape(1, LANE)
    sign_row = jnp.tile(jnp.concatenate([-jnp.ones(half, F32), jnp.ones(half, F32)]), LANE // HEAD_DIM).reshape(1, LANE)
    pos = positions.reshape(t, 1)
    row = pl.BlockSpec((tm, LANE), lambda i: (i, 0))
    return pl.pallas_call(
        _rope_table_body,
        grid=(t // tm,),
        in_specs=[pl.BlockSpec((tm, 1), lambda i: (i, 0)), _const_spec((1, LANE)), _const_spec((1, LANE))],
        out_specs=[row, row],
        out_shape=[jax.ShapeDtypeStruct((t, LANE), F32)] * 2,
        compiler_params=_params("parallel"),
        name="rope_tables",
    )(pos, inv_row, sign_row)


def _rope(x, cos, sin):
    w = x.shape[-1]
    reps = w // LANE
    if reps > 1:
        cos = jnp.concatenate([cos] * reps, axis=-1)
        sin = jnp.concatenate([sin] * reps, axis=-1)
    lane = lax.broadcasted_iota(jnp.int32, x.shape, 1)
    first_half = (lane & (HEAD_DIM // 2)) == 0
    swapped = jnp.where(first_half, pltpu.roll(x, w - HEAD_DIM // 2, 1), pltpu.roll(x, HEAD_DIM // 2, 1))
    return x * cos + swapped * sin


def _ffn_body(*refs, d_ff, d_model, with_mix):
    if with_mix:
        x_ref, r_ref, m_ref, a_ref, wr_ref, wm_ref, wa_ref, g_ref, wgu_ref, wd_ref, o_ref, act_ref, x_mid_ref = refs
        r, m, a = r_ref[...], m_ref[...], a_ref[...]
        for c in range(d_model // MXU_N):
            cols = slice(c * MXU_N, (c + 1) * MXU_N)
            x_mid_ref[:, cols] = x_ref[:, cols] + (
                _dot(r, wr_ref[:, cols]) + _dot(m, wm_ref[:, cols]) + _dot(a, wa_ref[:, cols]))
        x_ref = x_mid_ref
    else:
        x_ref, g_ref, wgu_ref, wd_ref, o_ref, act_ref = refs
    xn = _rms(x_ref[...], g_ref[...]).astype(BF16)
    for c in range(d_ff // MXU_N):
        cols = slice(c * MXU_N, (c + 1) * MXU_N)
        ucols = slice(d_ff + c * MXU_N, d_ff + (c + 1) * MXU_N)
        g = _dot(xn, wgu_ref[:, cols])
        u = _dot(xn, wgu_ref[:, ucols])
        act_ref[:, cols] = (_silu(g) * u).astype(BF16)
    act = act_ref[...]
    for c in range(d_model // MXU_N):
        cols = slice(c * MXU_N, (c + 1) * MXU_N)
        o_ref[:, cols] = x_ref[:, cols] + 0.5 * _dot(act, wd_ref[:, cols])


def _ffn(x, norm, w_gate_up, w_down, tm, mix=None):
    t, d_model = x.shape
    d_ff = w_down.shape[0]
    row = lambda n: pl.BlockSpec((tm, n), lambda i: (i, 0))
    operands, specs, scratch = [x], [row(d_model)], [pltpu.VMEM((tm, d_ff), BF16)]
    if mix is not None:
        ret, mla, gla, w_out = mix
        nr, nm, na = ret.shape[1], mla.shape[1], gla.shape[1]
        w = w_out.astype(BF16)
        operands += [ret, mla, gla, w[:nr], w[nr:nr + nm], w[nr + nm:]]
        specs += [row(nr), row(nm), row(na), _const_spec((nr, d_model)), _const_spec((nm, d_model)),
                  _const_spec((na, d_model))]
        scratch.append(pltpu.VMEM((tm, d_model), F32))
    operands += [norm.reshape(1, d_model), w_gate_up.astype(BF16), w_down.astype(BF16)]
    specs += [_const_spec((1, d_model)), _const_spec((d_model, 2 * d_ff)), _const_spec((d_ff, d_model))]
    return pl.pallas_call(
        functools.partial(_ffn_body, d_ff=d_ff, d_model=d_model, with_mix=mix is not None),
        grid=(t // tm,),
        in_specs=specs,
        out_specs=row(d_model),
        out_shape=jax.ShapeDtypeStruct((t, d_model), F32),
        scratch_shapes=scratch,
        compiler_params=_params("parallel"),
        name="mix_ffn" if mix is not None else "ffn",
    )(*operands)


W_RET = 0
W_CQ = 4 * GROUP
W_TAIL = W_CQ + MLA_Q_RANK
W_CKV = W_TAIL + LANE
W_GLA = W_CKV + MLA_KV_RANK
W_GLA_R = W_GLA + 3 * GROUP
W_COLS = W_GLA_R + GROUP


def _inproj_body(x_ref, g_ref, cos_ref, sin_ref, w_ref, qn_ref, wuq_ref, kvn_ref, wukv_ref,
                 gqn_ref, gqr_ref, gkn_ref, gkr_ref, wga_ref, ba_ref,
                 rq_ref, rk_ref, rv_ref, rg_ref, mq_ref, mk_ref, mv_ref,
                 gq_ref, gk_ref, gv_ref, la_ref, gg_ref):
    cos = cos_ref[...]
    sin = sin_ref[...]
    xn = _rms(x_ref[...], g_ref[...]).astype(BF16)
    tm = xn.shape[0]
    lane = lax.broadcasted_iota(jnp.int32, (tm, LANE), 1)
    low_half = lane < HEAD_DIM
    dk_scale = HEAD_DIM ** -0.5

    def proj(lo, width):
        return _dot(xn, w_ref[:, lo:lo + width])

    rq_ref[...] = _rope(proj(W_RET, GROUP), cos, sin).astype(BF16)
    rk_ref[...] = (_rope(proj(W_RET + GROUP, GROUP), cos, sin) * dk_scale).astype(BF16)
    rv_ref[...] = proj(W_RET + 2 * GROUP, GROUP).astype(BF16)
    rg_ref[...] = _silu(proj(W_RET + 3 * GROUP, GROUP)).astype(BF16)

    gq_ref[...] = proj(W_GLA, GROUP) * dk_scale
    gk_ref[...] = proj(W_GLA + GROUP, GROUP)
    gv_ref[...] = proj(W_GLA + 2 * GROUP, GROUP).astype(BF16)
    gg_ref[...] = _silu(proj(W_GLA_R, GROUP)).astype(BF16)
    cq_tail = proj(W_CQ, MLA_Q_RANK + LANE)
    tail = cq_tail[:, MLA_Q_RANK:]
    z = _dot(tail.astype(BF16), wga_ref[...]) + ba_ref[...]
    la_ref[...] = (jnp.minimum(z, 0.0) - jnp.log(1.0 + jnp.exp(-jnp.abs(z)))) * (1.0 / GLA_TAU)

    kr_ms = jnp.sum(jnp.where(low_half, tail * tail, 0.0), axis=-1, keepdims=True) * (1.0 / MLA_ROPE)
    kr = _rope(tail * lax.rsqrt(kr_ms + EPS) * gkr_ref[...], cos, sin)
    kr_t = kr.T.astype(BF16)
    ckv = _rms(proj(W_CKV, MLA_KV_RANK), kvn_ref[...]).astype(BF16)
    ones_col = jnp.where(lane == 0, 1.0, 0.0).astype(BF16)
    v_base = N_HEADS * MLA_NOPE
    for pair in range(N_HEADS // 2):
        kn2 = _dot(ckv, wukv_ref[:, pair * MXU_N:(pair + 1) * MXU_N])
        v2 = _dot(ckv, wukv_ref[:, v_base + pair * MXU_N:v_base + (pair + 1) * MXU_N])
        for sub in range(2):
            h = 2 * pair + sub
            kn = kn2[:, sub * MLA_NOPE:(sub + 1) * MLA_NOPE]
            mk_ref[h, 0:MLA_NOPE, :] = _rms(kn, gkn_ref[...]).T.astype(BF16)
            mk_ref[h, MLA_NOPE:MLA_SLOT, :] = kr_t
            mv_ref[:, h * MLA_SLOT:h * MLA_SLOT + MLA_DV] = v2[:, sub * MLA_DV:(sub + 1) * MLA_DV].astype(BF16)
            mv_ref[:, h * MLA_SLOT + MLA_DV:(h + 1) * MLA_SLOT] = ones_col

    sm_scale = (MLA_NOPE + MLA_ROPE) ** -0.5 * math.log2(math.e)
    cq = _rms(cq_tail[:, :MLA_Q_RANK], qn_ref[...]).astype(BF16)
    for pair in range(N_HEADS // 2):
        qn2 = _dot(cq, wuq_ref[:, pair * MXU_N:(pair + 1) * MXU_N])
        for sub in range(2):
            h = 2 * pair + sub
            qn = qn2[:, sub * MLA_NOPE:(sub + 1) * MLA_NOPE]
            mq_ref[:, h * MLA_SLOT:h * MLA_SLOT + MLA_NOPE] = (_rms(qn, gqn_ref[...]) * sm_scale).astype(BF16)
    qr4 = _dot(cq, wuq_ref[:, N_HEADS * MLA_NOPE:N_HEADS * MLA_NOPE + N_HEADS * MLA_ROPE])
    for pair in range(N_HEADS // 2):
        qr = qr4[:, pair * LANE:(pair + 1) * LANE]
        q2 = qr * qr
        ms_lo = jnp.sum(jnp.where(low_half, q2, 0.0), axis=-1, keepdims=True)
        ms_hi = jnp.sum(jnp.where(low_half, 0.0, q2), axis=-1, keepdims=True)
        ms = jnp.where(low_half, ms_lo, ms_hi) * (1.0 / MLA_ROPE)
        qr = _rope(qr * lax.rsqrt(ms + EPS) * gqr_ref[...], cos, sin) * sm_scale
        h0 = 2 * pair
        mq_ref[:, h0 * MLA_SLOT + MLA_NOPE:(h0 + 1) * MLA_SLOT] = jnp.where(low_half, qr, 0.0).astype(BF16)
        mq_ref[:, (h0 + 1) * MLA_SLOT + MLA_NOPE:(h0 + 2) * MLA_SLOT] = jnp.where(
            low_half, pltpu.roll(qr, HEAD_DIM, 1), 0.0).astype(BF16)


def _reorder_w_in(w_in):
    d = w_in.shape[0]
    o = np.cumsum([0, GROUP, GROUP, GROUP, GROUP, MLA_Q_RANK, MLA_KV_RANK, MLA_ROPE,
                   GROUP, GROUP, GROUP, GLA_GATE_RANK, GROUP])
    seg = [w_in[:, o[i]:o[i + 1]] for i in range(12)]
    ret_q, ret_k, ret_v, ret_g, cq, ckv, kr, gq, gk, gv, a_low, gr = seg
    pad = jnp.zeros((d, LANE - MLA_ROPE - GLA_GATE_RANK), w_in.dtype)
    return jnp.concatenate([ret_q, ret_k, ret_v, ret_g, cq, kr, a_low, pad, ckv, gq, gk, gv, gr], axis=1)


def _inproj(x, cos, sin, norm, w_in, mla_q_norm, mla_w_uq, mla_kv_norm, mla_w_ukv,
            q_nope_norm, q_rope_norm, k_nope_norm, k_rope_norm, gla_w_gate_up, gla_gate_bias, batch, tm):
    t, d = x.shape
    seq = t // batch
    tiles_per_seq = seq // tm
    w = _reorder_w_in(w_in).astype(BF16)
    qd = MLA_NOPE + MLA_ROPE
    wuq = mla_w_uq.reshape(MLA_Q_RANK, N_HEADS, qd)
    wuq = jnp.concatenate([wuq[:, :, :MLA_NOPE].reshape(MLA_Q_RANK, -1),
                           wuq[:, :, MLA_NOPE:].reshape(MLA_Q_RANK, -1)], axis=1).astype(BF16)
    wukv = mla_w_ukv.reshape(MLA_KV_RANK, N_HEADS, MLA_NOPE + MLA_DV)
    wukv = jnp.concatenate([wukv[:, :, :MLA_NOPE].reshape(MLA_KV_RANK, -1),
                            wukv[:, :, MLA_NOPE:].reshape(MLA_KV_RANK, -1)], axis=1).astype(BF16)
    gqr = jnp.tile(q_rope_norm, LANE // MLA_ROPE).reshape(1, LANE)
    gkr = jnp.concatenate([k_rope_norm, jnp.zeros(LANE - MLA_ROPE, F32)]).reshape(1, LANE)
    wga = jnp.zeros((LANE, GROUP), F32).at[MLA_ROPE:MLA_ROPE + GLA_GATE_RANK].set(gla_w_gate_up).astype(BF16)

    row = lambda n: pl.BlockSpec((tm, n), lambda i: (i, 0))
    out = lambda n, dt: jax.ShapeDtypeStruct((t, n), dt)
    wide = N_HEADS * MLA_SLOT
    keys_t = pl.BlockSpec((None, N_HEADS, MLA_SLOT, tm), lambda i: (i // tiles_per_seq, 0, 0, i % tiles_per_seq))
    return pl.pallas_call(
        _inproj_body,
        grid=(t // tm,),
        in_specs=[row(d), _const_spec((1, d)), row(LANE), row(LANE), _const_spec((d, W_COLS)),
                  _const_spec((1, MLA_Q_RANK)), _const_spec(wuq.shape), _const_spec((1, MLA_KV_RANK)),
                  _const_spec(wukv.shape), _const_spec((1, MLA_NOPE)), _const_spec((1, LANE)),
                  _const_spec((1, MLA_NOPE)), _const_spec((1, LANE)), _const_spec((LANE, GROUP)),
                  _const_spec((1, GROUP))],
        out_specs=[row(GROUP)] * 4 + [row(wide), keys_t, row(wide)] + [row(GROUP)] * 5,
        out_shape=[out(GROUP, BF16)] * 4
                  + [out(wide, BF16), jax.ShapeDtypeStruct((batch, N_HEADS, MLA_SLOT, seq), BF16), out(wide, BF16)]
                  + [out(GROUP, F32), out(GROUP, F32), out(GROUP, BF16), out(GROUP, F32), out(GROUP, BF16)],
        compiler_params=_params("parallel"),
        name="inproj",
    )(x, norm.reshape(1, d), cos, sin, w, mla_q_norm.reshape(1, -1), wuq, mla_kv_norm.reshape(1, -1), wukv,
      q_nope_norm.reshape(1, -1), gqr, k_nope_norm.reshape(1, -1), gkr, wga, gla_gate_bias.reshape(1, -1))


def _head_masks(shape):
    lane = lax.broadcasted_iota(jnp.int32, shape, 1)
    return [(lane >> int(math.log2(HEAD_DIM))) == h for h in range(N_HEADS)]


def _stack_heads(x, stack_mask):
    return jnp.concatenate([x] * N_HEADS, axis=0) * stack_mask


def _unstack_heads(y, masks, c):
    out = y[(N_HEADS - 1) * c:N_HEADS * c]
    for h in range(N_HEADS - 2, -1, -1):
        out = jnp.where(masks[h], y[h * c:(h + 1) * c], out)
    return out


def _stack_mask_const(chunk):
    head = np.arange(GROUP) // HEAD_DIM
    block = np.repeat(np.arange(N_HEADS), chunk)
    return jnp.asarray((block[:, None] == head[None, :]).astype(np.float32), BF16)


def _head_norm_gate(o, gate, gmat, norm):
    ms = _dot((o * o).astype(BF16), gmat) * (1.0 / HEAD_DIM)
    return (gate.astype(F32) * (o * lax.rsqrt(ms + EPS) * norm)).astype(BF16)


def _block_diag(dtype):
    idx = np.arange(GROUP) // HEAD_DIM
    return jnp.asarray((idx[:, None] == idx[None, :]).astype(np.float32), dtype)


def _ret_body(q_ref, k_ref, v_ref, g_ref, intra_ref, xi_ref, zeta_ref, cd_ref, gmat_ref, smask_ref, norm_ref,
              o_ref, state_ref, *, n_chunks):
    @pl.when(pl.program_id(1) == 0)
    def _():
        state_ref[...] = jnp.zeros_like(state_ref)

    masks = _head_masks((RET_CHUNK, GROUP))
    for c in range(n_chunks):
        rows = slice(c * RET_CHUNK, (c + 1) * RET_CHUNK)
        q = q_ref[rows, :]
        k = k_ref[rows, :]
        v = v_ref[rows, :]
        sc = lax.dot_general(_stack_heads(q, smask_ref[...]), k, NT_DIMS, preferred_element_type=F32) * intra_ref[...]
        o = _unstack_heads(_dot(sc.astype(BF16), v), masks, RET_CHUNK)
        state = state_ref[...]
        o = o + _dot(q, state.astype(BF16)) * xi_ref[...]
        vz = (v.astype(F32) * zeta_ref[...]).astype(BF16)
        kv = lax.dot_general(k, vz, TN_DIMS, preferred_element_type=F32)
        state_ref[...] = state * cd_ref[...] + kv * (cd_ref[...] > 0.0).astype(F32)
        o_ref[rows, :] = _head_norm_gate(o, g_ref[rows, :], gmat_ref[...], norm_ref[...])


def _retention(q, k, v, gate, norm, batch, rows_per_step):
    t = q.shape[0]
    seq = t // batch
    steps = seq // rows_per_step
    log_g = np.log1p(-np.power(2.0, -5.0 - np.arange(N_HEADS, dtype=np.float64)))
    chunk = RET_CHUNK
    idx = np.arange(chunk, dtype=np.float64)
    diff = idx[:, None] - idx[None, :]
    intra = np.where(diff >= 0, np.exp(log_g[:, None, None] * np.maximum(diff, 0.0)), 0.0)
    intra = intra.reshape(N_HEADS * chunk, chunk)
    lane_g = np.repeat(log_g, HEAD_DIM)[None, :]
    xi = np.exp(lane_g * (idx[:, None] + 1.0))
    zeta = np.exp(lane_g * (chunk - 1.0 - idx[:, None]))
    head = np.arange(GROUP) // HEAD_DIM
    cd = (head[:, None] == head[None, :]) * np.exp(lane_g * chunk)
    consts = [jnp.asarray(a, F32) for a in (intra, xi, zeta, cd)]
    row = pl.BlockSpec((rows_per_step, GROUP), lambda b, i: (b * steps + i, 0))
    return pl.pallas_call(
        functools.partial(_ret_body, n_chunks=rows_per_step // chunk),
        grid=(batch, steps),
        in_specs=[row] * 4 + [_const_spec(c.shape) for c in consts]
                 + [_const_spec((GROUP, GROUP)), _const_spec((N_HEADS * chunk, GROUP)), _const_spec((1, GROUP))],
        out_specs=row,
        out_shape=jax.ShapeDtypeStruct((t, GROUP), BF16),
        scratch_shapes=[pltpu.VMEM((GROUP, GROUP), F32)],
        compiler_params=_params("parallel", "arbitrary"),
        name="retention",
    )(q, k, v, gate, *consts, _block_diag(BF16), _stack_mask_const(chunk), norm.reshape(1, GROUP))


SUBLANES = 8


def _boundary_rows(b, m, row):
    if 2 * m >= SUBLANES:
        pieces = []
        for j in range(CHUNK // (2 * m)):
            r = (2 * j + 1) * m - 1
            pieces.append(jnp.broadcast_to(b[r:r + 1, :], (2 * m, GROUP)))
        return jnp.concatenate(pieces, axis=0)
    tiles = b.reshape(CHUNK // SUBLANES, SUBLANES, GROUP)
    sub = (row & (SUBLANES - 1)).reshape(tiles.shape)
    out = None
    for j in range(SUBLANES // (2 * m) - 1, -1, -1):
        r = (2 * j + 1) * m - 1
        piece = jnp.broadcast_to(tiles[:, r:r + 1, :], tiles.shape)
        out = piece if out is None else jnp.where(sub < (2 * j + 2) * m, piece, out)
    return out.reshape(CHUNK, GROUP)


def _gla_body(q_ref, k_ref, v_ref, a_ref, g_ref, ltri_ref, lmask_ref, gmat_ref, bd_ref, qmask_ref, kmask_ref, norm_ref,
              o_ref, state_ref, *, n_chunks):
    @pl.when(pl.program_id(1) == 0)
    def _():
        state_ref[...] = jnp.zeros_like(state_ref)

    masks = _head_masks((CHUNK, GROUP))
    row = lax.broadcasted_iota(jnp.int32, (CHUNK, GROUP), 0)
    gmat = gmat_ref[...]
    ltri = ltri_ref[...]
    for c in range(n_chunks):
        rows = slice(c * CHUNK, (c + 1) * CHUNK)
        q = q_ref[rows, :]
        k = k_ref[rows, :]
        vb = v_ref[rows, :]
        v = vb.astype(F32)
        a = a_ref[rows, :]
        a0 = a.astype(BF16)
        r1 = a - a0.astype(F32)
        a1 = r1.astype(BF16)
        a2 = (r1 - a1.astype(F32)).astype(BF16)
        b = _dot(ltri, a0) + _dot(ltri, a1) + _dot(ltri, a2)
        b_last = b[CHUNK - 1:CHUNK, :]

        state = state_ref[...]
        o = lax.dot_general((q * jnp.exp(b)).astype(BF16), state.astype(BF16), NT_DIMS, preferred_element_type=F32)
        kd = (k * jnp.exp(b_last - b)).astype(BF16)
        kv = lax.dot_general(vb, kd, TN_DIMS, preferred_element_type=F32)
        state_ref[...] = state * jnp.exp(b_last) + kv * bd_ref[...]

        sc = None
        for li, m in enumerate(GLA_LEVELS):
            bref = _boundary_rows(b, m, row)
            upper = ((row >> int(math.log2(m))) & 1) == 1
            w = jnp.exp(-jnp.abs(b - bref))
            scaled = (jnp.where(upper, q, k) * w).astype(BF16)
            qs = _stack_heads(scaled, qmask_ref[li])
            kt = scaled * kmask_ref[li]
            term = lax.dot_general(qs, kt, NT_DIMS, preferred_element_type=F32)
            if 2 * m < CHUNK:
                term = term * lmask_ref[li]
            sc = term if sc is None else sc + term
        o = o + _unstack_heads(_dot(sc.astype(BF16), vb), masks, CHUNK)

        qk = q * k
        qk_hi = qk.astype(BF16)
        prods = [qk_hi, (qk - qk_hi.astype(F32)).astype(BF16)]
        for d in range(1, LEAF):
            valid = (row & (LEAF - 1)) >= d
            w = jnp.exp(jnp.where(valid, b - pltpu.roll(b, d, 0), NEG))
            prods.append((q * pltpu.roll(k, d, 0) * w).astype(BF16))
        sums = _dot(jnp.concatenate(prods, axis=0), gmat)
        o = o + (sums[0:CHUNK] + sums[CHUNK:2 * CHUNK]) * v
        for d in range(1, LEAF):
            o = o + sums[(d + 1) * CHUNK:(d + 2) * CHUNK] * pltpu.roll(v, d, 0)

        o_ref[rows, :] = _head_norm_gate(o, g_ref[rows, :], gmat, norm_ref[...])


def _gla(q, k, v, log_a, gate, norm, batch, rows_per_step):
    t = q.shape[0]
    seq = t // batch
    steps = seq // rows_per_step
    idx = np.arange(CHUNK)
    ltri = jnp.asarray(idx[:, None] >= idx[None, :], BF16)
    lm = []
    for m in GLA_LEVELS:
        same = (idx[:, None] // (2 * m)) == (idx[None, :] // (2 * m))
        ok = same & (((idx[:, None] // m) & 1) == 1) & (((idx[None, :] // m) & 1) == 0)
        lm.append(np.tile(ok, (N_HEADS, 1)))
    lmask = jnp.asarray(np.stack(lm), F32)
    upper = np.stack([((idx // m) & 1) == 1 for m in GLA_LEVELS]).astype(np.float32)
    head = np.arange(GROUP) // HEAD_DIM
    stack = (np.repeat(np.arange(N_HEADS), CHUNK)[:, None] == head[None, :]).astype(np.float32)
    qmask = jnp.asarray(np.tile(upper, (1, N_HEADS))[:, :, None] * stack[None], BF16)
    kmask = jnp.asarray(np.broadcast_to((1.0 - upper)[:, :, None], (len(GLA_LEVELS), CHUNK, GROUP)), BF16)
    row = pl.BlockSpec((rows_per_step, GROUP), lambda b, i: (b * steps + i, 0))
    return pl.pallas_call(
        functools.partial(_gla_body, n_chunks=rows_per_step // CHUNK),
        grid=(batch, steps),
        in_specs=[row] * 5 + [_const_spec((CHUNK, CHUNK)), _const_spec(lmask.shape), _const_spec((GROUP, GROUP)),
                              _const_spec((GROUP, GROUP)), _const_spec(qmask.shape), _const_spec(kmask.shape),
                              _const_spec((1, GROUP))],
        out_specs=row,
        out_shape=jax.ShapeDtypeStruct((t, GROUP), BF16),
        scratch_shapes=[pltpu.VMEM((GROUP, GROUP), F32)],
        compiler_params=_params("parallel", "arbitrary"),
        name="gla",
    )(q, k, v, log_a, gate, ltri, lmask, _block_diag(BF16), _block_diag(F32), qmask, kmask, norm.reshape(1, GROUP))


def _mla_body(q_ref, k_ref, v_ref, o_ref, s0_ref, s1_ref, bm0_ref, bm1_ref, m_ref, acc_ref, *, tq, tk):
    i = pl.program_id(2)
    r = tq // tk
    q = q_ref[...]
    slots = ((s0_ref, bm0_ref), (s1_ref, bm1_ref))

    def scores(blk):
        start = pl.multiple_of(blk * tk, tk)
        return _dot(q, k_ref[:, pl.ds(start, tk)])

    def stash(slot, s):
        s_ref, bm_ref = slots[slot]
        s_ref[...] = s
        bm_ref[...] = jnp.broadcast_to(jnp.max(s, axis=-1, keepdims=True), bm_ref.shape)

    def absorb(slot, blk):
        s_ref, bm_ref = slots[slot]
        start = pl.multiple_of(blk * tk, tk)
        m_prev = m_ref[...]
        m_new = jnp.maximum(m_prev, bm_ref[...])
        m_wide = jnp.concatenate([m_new] * (tk // LANE), axis=1)
        p = jnp.exp2(s_ref[...] - m_wide).astype(BF16)
        alpha = jnp.exp2(m_prev - m_new)
        alpha_wide = jnp.concatenate([alpha] * (MLA_SLOT // LANE), axis=1)
        acc_ref[...] = acc_ref[...] * alpha_wide + _dot(p, v_ref[pl.ds(start, tk), :])
        m_ref[...] = m_new

    def block_at(n):
        return jnp.where(n < r, i * r + n, i * r + (r - 1) - n)

    m_ref[...] = jnp.full(m_ref.shape, NEG, F32)
    acc_ref[...] = jnp.zeros_like(acc_ref)
    row = lax.broadcasted_iota(jnp.int32, (tq, tk), 0)
    col = lax.broadcasted_iota(jnp.int32, (tq, tk), 1)
    for n in range(r):
        stash(n & 1, jnp.where(col + n * tk <= row, scores(i * r + n), NEG))
        if n > 0:
            absorb((n - 1) & 1, i * r + n - 1)

    def body(n, carry):
        for slot in (0, 1):
            @pl.when((n & 1) == slot)
            def _():
                stash(slot, scores(block_at(n)))
                absorb(1 - slot, block_at(n - 1))
        return carry

    steps = (i + 1) * r
    lax.fori_loop(r, steps, body, 0)
    for slot in (0, 1):
        @pl.when(((steps - 1) & 1) == slot)
        def _():
            absorb(slot, block_at(steps - 1))
    acc = acc_ref[...]
    o_ref[...] = (acc[:, :MLA_DV] / acc[:, MLA_DV:MLA_DV + 1]).astype(BF16)


def _mla_attention(q, k, v, batch, tq, tk):
    t = q.shape[0]
    seq = t // batch
    nq = seq // tq
    q3, v3 = (a.reshape(batch, seq, N_HEADS * MLA_SLOT) for a in (q, v))
    k_spec = pl.BlockSpec((None, None, MLA_SLOT, seq), lambda b, h, i: (b, h, 0, 0))
    v_spec = pl.BlockSpec((None, seq, MLA_SLOT), lambda b, h, i: (b, 0, h))
    out = pl.pallas_call(
        functools.partial(_mla_body, tq=tq, tk=tk),
        grid=(batch, N_HEADS, nq),
        in_specs=[pl.BlockSpec((None, tq, MLA_SLOT), lambda b, h, i: (b, i, h)), k_spec, v_spec],
        out_specs=pl.BlockSpec((None, tq, MLA_DV), lambda b, h, i: (b, i, h)),
        out_shape=jax.ShapeDtypeStruct((batch, seq, N_HEADS * MLA_DV), BF16),
        scratch_shapes=[pltpu.VMEM((tq, tk), F32), pltpu.VMEM((tq, tk), F32), pltpu.VMEM((tq, LANE), F32),
                        pltpu.VMEM((tq, LANE), F32), pltpu.VMEM((tq, LANE), F32), pltpu.VMEM((tq, MLA_SLOT), F32)],
        compiler_params=_params("parallel", "parallel", "arbitrary"),
        name="mla_attention",
    )(q3, k, v3)
    return out.reshape(t, N_HEADS * MLA_DV)


def _tiles(batch, seq):
    tm = min(512, seq)
    tm_ffn = min(1024, seq)
    rows_per_step = min(1024, seq)
    tk = min(1024, seq)
    tq = tk
    return tm, tm_ffn, rows_per_step, tq, tk


def kernel(x, positions, ffn1_norm, ffn1_w_gate_up, ffn1_w_down, mix_norm, w_in, ret_out_norm, mla_q_norm, mla_w_uq, mla_kv_norm, mla_w_ukv, mla_q_nope_norm, mla_q_rope_norm, mla_k_nope_norm, mla_k_rope_norm, gla_w_gate_up, gla_gate_bias, gla_out_norm, w_out, ffn2_norm, ffn2_w_gate_up, ffn2_w_down):
    batch, seq, d_model = x.shape
    depth = w_in.shape[0]
    tm, tm_ffn, rows_per_step, tq, tk = _tiles(batch, seq)
    xt = x.reshape(batch * seq, d_model)
    cos, sin = _rope_tables(positions, tm)
    for l in range(depth):
        xt = _ffn(xt, ffn1_norm[l], ffn1_w_gate_up[l], ffn1_w_down[l], tm_ffn)
        (rq, rk, rv, rg, mq, mk, mv, gq, gk, gv, la, gg) = _inproj(
            xt, cos, sin, mix_norm[l], w_in[l], mla_q_norm[l], mla_w_uq[l], mla_kv_norm[l], mla_w_ukv[l],
            mla_q_nope_norm[l], mla_q_rope_norm[l], mla_k_nope_norm[l], mla_k_rope_norm[l],
            gla_w_gate_up[l], gla_gate_bias[l], batch, tm)
        ret = _retention(rq, rk, rv, rg, ret_out_norm[l], batch, rows_per_step)
        mla = _mla_attention(mq, mk, mv, batch, tq, tk)
        gla = _gla(gq, gk, gv, la, gg, gla_out_norm[l], batch, rows_per_step)
        xt = _ffn(xt, ffn2_norm[l], ffn2_w_gate_up[l], ffn2_w_down[l], tm_ffn, mix=(ret, mla, gla, w_out[l]))
    return xt.reshape(batch, seq, d_model)
```

```python
import functools
import math

import numpy as np
import jax
import jax.numpy as jnp
from jax import lax
from jax.experimental import pallas as pl
from jax.experimental.pallas import tpu as pltpu

F32 = jnp.float32
BF16 = jnp.bfloat16

EPS = 1e-6
ROPE_THETA = 10000.0
HEAD_DIM = 64
N_HEADS = 4
GROUP = N_HEADS * HEAD_DIM
MLA_NOPE = 128
MLA_ROPE = 64
MLA_DV = 128
MLA_Q_RANK = 384
MLA_KV_RANK = 256
MLA_SLOT = 256
GLA_GATE_RANK = 16
GLA_TAU = 16.0
CHUNK = 128
RET_CHUNK = 256
LEAF = 1
GLA_LEVELS = (64, 32, 16, 8, 4, 2, 1)
NEG = -1e30

LANE = 128
MXU_N = 256
VMEM_LIMIT_BYTES = 56 * 1024 * 1024

NT_DIMS = (((1,), (1,)), ((), ()))
TN_DIMS = (((0,), (0,)), ((), ()))


def _params(*sem):
    return pltpu.CompilerParams(dimension_semantics=sem, vmem_limit_bytes=VMEM_LIMIT_BYTES)


def _const_spec(shape):
    nd = len(shape)
    return pl.BlockSpec(shape, lambda *_: (0,) * nd, pipeline_mode=pl.Buffered(1))


def _dot(a, b):
    return jnp.dot(a, b, preferred_element_type=F32)


def _rms(x, g):
    return x * lax.rsqrt(jnp.mean(x * x, axis=-1, keepdims=True) + EPS) * g


def _silu(x):
    return x * jax.nn.sigmoid(x)


WEIGHT_CAST_BLOCKS = 4


def _cast_body(w_ref, o_ref):
    o_ref[...] = w_ref[...].astype(BF16)


def _layer_bf16(stacked, layer):
    _, rows, cols = stacked.shape
    br = rows // WEIGHT_CAST_BLOCKS
    return pl.pallas_call(
        _cast_body,
        grid=(WEIGHT_CAST_BLOCKS,),
        in_specs=[pl.BlockSpec((None, br, cols), lambda i: (layer, i, 0))],
        out_specs=pl.BlockSpec((br, cols), lambda i: (i, 0)),
        out_shape=jax.ShapeDtypeStruct((rows, cols), BF16),
        compiler_params=_params("parallel"),
        name="weight_cast",
    )(stacked)


def _rope_table_body(pos_ref, inv_ref, sign_ref, cos_ref, sin_ref):
    ang = pos_ref[...].astype(F32) * inv_ref[...]
    cos_ref[...] = jnp.cos(ang)
    sin_ref[...] = jnp.sin(ang) * sign_ref[...]


def _rope_tables(positions, tm):
    t = positions.size
    half = HEAD_DIM // 2
    inv = ROPE_THETA ** (-jnp.arange(0, HEAD_DIM, 2, dtype=F32) / HEAD_DIM)
    inv_row = jnp.tile(inv, LANE // half).reshape(1, LANE)
    sign_row = jnp.tile(jnp.concatenate([-jnp.ones(half, F32), jnp.ones(half, F32)]), LANE // HEAD_DIM).reshape(1, LANE)
    pos = positions.reshape(t, 1)
    row = pl.BlockSpec((tm, LANE), lambda i: (i, 0))
    return pl.pallas_call(
        _rope_table_body,
        grid=(t // tm,),
        in_specs=[pl.BlockSpec((tm, 1), lambda i: (i, 0)), _const_spec((1, LANE)), _const_spec((1, LANE))],
        out_specs=[row, row],
        out_shape=[jax.ShapeDtypeStruct((t, LANE), F32)] * 2,
        compiler_params=_params("parallel"),
        name="rope_tables",
    )(pos, inv_row, sign_row)


def _rope(x, cos, sin):
    w = x.shape[-1]
    reps = w // LANE
    if reps > 1:
        cos = jnp.concatenate([cos] * reps, axis=-1)
        sin = jnp.concatenate([sin] * reps, axis=-1)
    lane = lax.broadcasted_iota(jnp.int32, x.shape, 1)
    first_half = (lane & (HEAD_DIM // 2)) == 0
    swapped = jnp.where(first_half, pltpu.roll(x, w - HEAD_DIM // 2, 1), pltpu.roll(x, HEAD_DIM // 2, 1))
    return x * cos + swapped * sin


def _ffn_body(*refs, d_ff, d_model, with_mix):
    if with_mix:
        x_ref, r_ref, m_ref, a_ref, wr_ref, wm_ref, wa_ref, g_ref, wgu_ref, wd_ref, o_ref, act_ref, x_mid_ref = refs
        r, m, a = r_ref[...], m_ref[...], a_ref[...]
        for c in range(d_model // MXU_N):
            cols = slice(c * MXU_N, (c + 1) * MXU_N)
            x_mid_ref[:, cols] = x_ref[:, cols] + (
                _dot(r, wr_ref[:, cols]) + _dot(m, wm_ref[:, cols]) + _dot(a, wa_ref[:, cols]))
        x_ref = x_mid_ref
    else:
        x_ref, g_ref, wgu_ref, wd_ref, o_ref, act_ref = refs
    xn = _rms(x_ref[...], g_ref[...]).astype(BF16)
    for c in range(d_ff // MXU_N):
        cols = slice(c * MXU_N, (c + 1) * MXU_N)
        ucols = slice(d_ff + c * MXU_N, d_ff + (c + 1) * MXU_N)
        g = _dot(xn, wgu_ref[:, cols])
        u = _dot(xn, wgu_ref[:, ucols])
        act_ref[:, cols] = (_silu(g) * u).astype(BF16)
    act = act_ref[...]
    for c in range(d_model // MXU_N):
        cols = slice(c * MXU_N, (c + 1) * MXU_N)
        o_ref[:, cols] = x_ref[:, cols] + 0.5 * _dot(act, wd_ref[:, cols])


def _ffn(x, norm, w_gate_up, w_down, tm, mix=None):
    t, d_model = x.shape
    d_ff = w_down.shape[0]
    row = lambda n: pl.BlockSpec((tm, n), lambda i: (i, 0))
    operands, specs, scratch = [x], [row(d_model)], [pltpu.VMEM((tm, d_ff), BF16)]
    if mix is not None:
        ret, mla, gla, w_out = mix
        nr, nm, na = ret.shape[1], mla.shape[1], gla.shape[1]
        w = w_out.astype(BF16)
        operands += [ret, mla, gla, w[:nr], w[nr:nr + nm], w[nr + nm:]]
        specs += [row(nr), row(nm), row(na), _const_spec((nr, d_model)), _const_spec((nm, d_model)),
                  _const_spec((na, d_model))]
        scratch.append(pltpu.VMEM((tm, d_model), F32))
    operands += [norm.reshape(1, d_model), w_gate_up.astype(BF16), w_down.astype(BF16)]
    specs += [_const_spec((1, d_model)), _const_spec((d_model, 2 * d_ff)), _const_spec((d_ff, d_model))]
    return pl.pallas_call(
        functools.partial(_ffn_body, d_ff=d_ff, d_model=d_model, with_mix=mix is not None),
        grid=(t // tm,),
        in_specs=specs,
        out_specs=row(d_model),
        out_shape=jax.ShapeDtypeStruct((t, d_model), F32),
        scratch_shapes=scratch,
        compiler_params=_params("parallel"),
        name="mix_ffn" if mix is not None else "ffn",
    )(*operands)


W_RET = 0
W_CQ = 4 * GROUP
W_TAIL = W_CQ + MLA_Q_RANK
W_CKV = W_TAIL + LANE
W_GLA = W_CKV + MLA_KV_RANK
W_GLA_R = W_GLA + 3 * GROUP
W_COLS = W_GLA_R + GROUP


def _inproj_body(x_ref, g_ref, cos_ref, sin_ref, w_ref, qn_ref, wuq_ref, kvn_ref, wukv_ref,
                 gqn_ref, gqr_ref, gkn_ref, gkr_ref, wga_ref, ba_ref,
                 rq_ref, rk_ref, rv_ref, rg_ref, mq_ref, mk_ref, mv_ref,
                 gq_ref, gk_ref, gv_ref, la_ref, gg_ref):
    cos = cos_ref[...]
    sin = sin_ref[...]
    xn = _rms(x_ref[...], g_ref[...]).astype(BF16)
    tm = xn.shape[0]
    lane = lax.broadcasted_iota(jnp.int32, (tm, LANE), 1)
    low_half = lane < HEAD_DIM
    dk_scale = HEAD_DIM ** -0.5

    def proj(lo, width):
        return _dot(xn, w_ref[:, lo:lo + width])

    rq_ref[...] = _rope(proj(W_RET, GROUP), cos, sin).astype(BF16)
    rk_ref[...] = (_rope(proj(W_RET + GROUP, GROUP), cos, sin) * dk_scale).astype(BF16)
    rv_ref[...] = proj(W_RET + 2 * GROUP, GROUP).astype(BF16)
    rg_ref[...] = _silu(proj(W_RET + 3 * GROUP, GROUP)).astype(BF16)

    gq_ref[...] = proj(W_GLA, GROUP) * dk_scale
    gk_ref[...] = proj(W_GLA + GROUP, GROUP)
    gv_ref[...] = proj(W_GLA + 2 * GROUP, GROUP).astype(BF16)
    gg_ref[...] = _silu(proj(W_GLA_R, GROUP)).astype(BF16)
    cq_tail = proj(W_CQ, MLA_Q_RANK + LANE)
    tail = cq_tail[:, MLA_Q_RANK:]
    z = _dot(tail.astype(BF16), wga_ref[...]) + ba_ref[...]
    la_ref[...] = (jnp.minimum(z, 0.0) - jnp.log(1.0 + jnp.exp(-jnp.abs(z)))) * (1.0 / GLA_TAU)

    kr_ms = jnp.sum(jnp.where(low_half, tail * tail, 0.0), axis=-1, keepdims=True) * (1.0 / MLA_ROPE)
    kr = _rope(tail * lax.rsqrt(kr_ms + EPS) * gkr_ref[...], cos, sin)
    kr_t = kr.T.astype(BF16)
    ckv = _rms(proj(W_CKV, MLA_KV_RANK), kvn_ref[...]).astype(BF16)
    ones_col = jnp.where(lane == 0, 1.0, 0.0).astype(BF16)
    v_base = N_HEADS * MLA_NOPE
    for pair in range(N_HEADS // 2):
        kn2 = _dot(ckv, wukv_ref[:, pair * MXU_N:(pair + 1) * MXU_N])
        v2 = _dot(ckv, wukv_ref[:, v_base + pair * MXU_N:v_base + (pair + 1) * MXU_N])
        for sub in range(2):
            h = 2 * pair + sub
            kn = kn2[:, sub * MLA_NOPE:(sub + 1) * MLA_NOPE]
            mk_ref[h, 0:MLA_NOPE, :] = _rms(kn, gkn_ref[...]).T.astype(BF16)
            mk_ref[h, MLA_NOPE:MLA_SLOT, :] = kr_t
            mv_ref[:, h * MLA_SLOT:h * MLA_SLOT + MLA_DV] = v2[:, sub * MLA_DV:(sub + 1) * MLA_DV].astype(BF16)
            mv_ref[:, h * MLA_SLOT + MLA_DV:(h + 1) * MLA_SLOT] = ones_col

    sm_scale = (MLA_NOPE + MLA_ROPE) ** -0.5 * math.log2(math.e)
    cq = _rms(cq_tail[:, :MLA_Q_RANK], qn_ref[...]).astype(BF16)
    for pair in range(N_HEADS // 2):
        qn2 = _dot(cq, wuq_ref[:, pair * MXU_N:(pair + 1) * MXU_N])
        for sub in range(2):
            h = 2 * pair + sub
            qn = qn2[:, sub * MLA_NOPE:(sub + 1) * MLA_NOPE]
            mq_ref[:, h * MLA_SLOT:h * MLA_SLOT + MLA_NOPE] = (_rms(qn, gqn_ref[...]) * sm_scale).astype(BF16)
    qr4 = _dot(cq, wuq_ref[:, N_HEADS * MLA_NOPE:N_HEADS * MLA_NOPE + N_HEADS * MLA_ROPE])
    for pair in range(N_HEADS // 2):
        qr = qr4[:, pair * LANE:(pair + 1) * LANE]
        q2 = qr * qr
        ms_lo = jnp.sum(jnp.where(low_half, q2, 0.0), axis=-1, keepdims=True)
        ms_hi = jnp.sum(jnp.where(low_half, 0.0, q2), axis=-1, keepdims=True)
        ms = jnp.where(low_half, ms_lo, ms_hi) * (1.0 / MLA_ROPE)
        qr = _rope(qr * lax.rsqrt(ms + EPS) * gqr_ref[...], cos, sin) * sm_scale
        h0 = 2 * pair
        mq_ref[:, h0 * MLA_SLOT + MLA_NOPE:(h0 + 1) * MLA_SLOT] = jnp.where(low_half, qr, 0.0).astype(BF16)
        mq_ref[:, (h0 + 1) * MLA_SLOT + MLA_NOPE:(h0 + 2) * MLA_SLOT] = jnp.where(
            low_half, pltpu.roll(qr, HEAD_DIM, 1), 0.0).astype(BF16)


def _reorder_w_in(w_in):
    d = w_in.shape[0]
    o = np.cumsum([0, GROUP, GROUP, GROUP, GROUP, MLA_Q_RANK, MLA_KV_RANK, MLA_ROPE,
                   GROUP, GROUP, GROUP, GLA_GATE_RANK, GROUP])
    seg = [w_in[:, o[i]:o[i + 1]] for i in range(12)]
    ret_q, ret_k, ret_v, ret_g, cq, ckv, kr, gq, gk, gv, a_low, gr = seg
    pad = jnp.zeros((d, LANE - MLA_ROPE - GLA_GATE_RANK), w_in.dtype)
    return jnp.concatenate([ret_q, ret_k, ret_v, ret_g, cq, kr, a_low, pad, ckv, gq, gk, gv, gr], axis=1)


def _inproj(x, cos, sin, norm, w_in, mla_q_norm, mla_w_uq, mla_kv_norm, mla_w_ukv,
            q_nope_norm, q_rope_norm, k_nope_norm, k_rope_norm, gla_w_gate_up, gla_gate_bias, batch, tm):
    t, d = x.shape
    seq = t // batch
    tiles_per_seq = seq // tm
    w = _reorder_w_in(w_in).astype(BF16)
    qd = MLA_NOPE + MLA_ROPE
    wuq = mla_w_uq.reshape(MLA_Q_RANK, N_HEADS, qd)
    wuq = jnp.concatenate([wuq[:, :, :MLA_NOPE].reshape(MLA_Q_RANK, -1),
                           wuq[:, :, MLA_NOPE:].reshape(MLA_Q_RANK, -1)], axis=1).astype(BF16)
    wukv = mla_w_ukv.reshape(MLA_KV_RANK, N_HEADS, MLA_NOPE + MLA_DV)
    wukv = jnp.concatenate([wukv[:, :, :MLA_NOPE].reshape(MLA_KV_RANK, -1),
                            wukv[:, :, MLA_NOPE:].reshape(MLA_KV_RANK, -1)], axis=1).astype(BF16)
    gqr = jnp.tile(q_rope_norm, LANE // MLA_ROPE).reshape(1, LANE)
    gkr = jnp.concatenate([k_rope_norm, jnp.zeros(LANE - MLA_ROPE, F32)]).reshape(1, LANE)
    wga = jnp.zeros((LANE, GROUP), F32).at[MLA_ROPE:MLA_ROPE + GLA_GATE_RANK].set(gla_w_gate_up).astype(BF16)

    row = lambda n: pl.BlockSpec((tm, n), lambda i: (i, 0))
    out = lambda n, dt: jax.ShapeDtypeStruct((t, n), dt)
    wide = N_HEADS * MLA_SLOT
    keys_t = pl.BlockSpec((None, N_HEADS, MLA_SLOT, tm), lambda i: (i // tiles_per_seq, 0, 0, i % tiles_per_seq))
    return pl.pallas_call(
        _inproj_body,
        grid=(t // tm,),
        in_specs=[row(d), _const_spec((1, d)), row(LANE), row(LANE), _const_spec((d, W_COLS)),
                  _const_spec((1, MLA_Q_RANK)), _const_spec(wuq.shape), _const_spec((1, MLA_KV_RANK)),
                  _const_spec(wukv.shape), _const_spec((1, MLA_NOPE)), _const_spec((1, LANE)),
                  _const_spec((1, MLA_NOPE)), _const_spec((1, LANE)), _const_spec((LANE, GROUP)),
                  _const_spec((1, GROUP))],
        out_specs=[row(GROUP)] * 4 + [row(wide), keys_t, row(wide)] + [row(GROUP)] * 5,
        out_shape=[out(GROUP, BF16)] * 4
                  + [out(wide, BF16), jax.ShapeDtypeStruct((batch, N_HEADS, MLA_SLOT, seq), BF16), out(wide, BF16)]
                  + [out(GROUP, F32), out(GROUP, F32), out(GROUP, BF16), out(GROUP, F32), out(GROUP, BF16)],
        compiler_params=_params("parallel"),
        name="inproj",
    )(x, norm.reshape(1, d), cos, sin, w, mla_q_norm.reshape(1, -1), wuq, mla_kv_norm.reshape(1, -1), wukv,
      q_nope_norm.reshape(1, -1), gqr, k_nope_norm.reshape(1, -1), gkr, wga, gla_gate_bias.reshape(1, -1))


def _head_masks(shape):
    lane = lax.broadcasted_iota(jnp.int32, shape, 1)
    return [(lane >> int(math.log2(HEAD_DIM))) == h for h in range(N_HEADS)]


def _stack_heads(x, stack_mask):
    return jnp.concatenate([x] * N_HEADS, axis=0) * stack_mask


def _unstack_heads(y, masks, c):
    out = y[(N_HEADS - 1) * c:N_HEADS * c]
    for h in range(N_HEADS - 2, -1, -1):
        out = jnp.where(masks[h], y[h * c:(h + 1) * c], out)
    return out


def _stack_mask_const(chunk):
    head = np.arange(GROUP) // HEAD_DIM
    block = np.repeat(np.arange(N_HEADS), chunk)
    return jnp.asarray((block[:, None] == head[None, :]).astype(np.float32), BF16)


def _head_norm_gate(o, gate, gmat, norm):
    ms = _dot((o * o).astype(BF16), gmat) * (1.0 / HEAD_DIM)
    return (gate.astype(F32) * (o * lax.rsqrt(ms + EPS) * norm)).astype(BF16)


def _block_diag(dtype):
    idx = np.arange(GROUP) // HEAD_DIM
    return jnp.asarray((idx[:, None] == idx[None, :]).astype(np.float32), dtype)


def _ret_body(q_ref, k_ref, v_ref, g_ref, intra_ref, xi_ref, zeta_ref, cd_ref, gmat_ref, smask_ref, norm_ref,
              o_ref, state_ref, *, n_chunks):
    @pl.when(pl.program_id(1) == 0)
    def _():
        state_ref[...] = jnp.zeros_like(state_ref)

    masks = _head_masks((RET_CHUNK, GROUP))
    for c in range(n_chunks):
        rows = slice(c * RET_CHUNK, (c + 1) * RET_CHUNK)
        q = q_ref[rows, :]
        k = k_ref[rows, :]
        v = v_ref[rows, :]
        sc = lax.dot_general(_stack_heads(q, smask_ref[...]), k, NT_DIMS, preferred_element_type=F32) * intra_ref[...]
        o = _unstack_heads(_dot(sc.astype(BF16), v), masks, RET_CHUNK)
        state = state_ref[...]
        o = o + _dot(q, state.astype(BF16)) * xi_ref[...]
        vz = (v.astype(F32) * zeta_ref[...]).astype(BF16)
        kv = lax.dot_general(k, vz, TN_DIMS, preferred_element_type=F32)
        state_ref[...] = state * cd_ref[...] + kv * (cd_ref[...] > 0.0).astype(F32)
        o_ref[rows, :] = _head_norm_gate(o, g_ref[rows, :], gmat_ref[...], norm_ref[...])


def _retention(q, k, v, gate, norm, batch, rows_per_step):
    t = q.shape[0]
    seq = t // batch
    steps = seq // rows_per_step
    log_g = np.log1p(-np.power(2.0, -5.0 - np.arange(N_HEADS, dtype=np.float64)))
    chunk = RET_CHUNK
    idx = np.arange(chunk, dtype=np.float64)
    diff = idx[:, None] - idx[None, :]
    intra = np.where(diff >= 0, np.exp(log_g[:, None, None] * np.maximum(diff, 0.0)), 0.0)
    intra = intra.reshape(N_HEADS * chunk, chunk)
    lane_g = np.repeat(log_g, HEAD_DIM)[None, :]
    xi = np.exp(lane_g * (idx[:, None] + 1.0))
    zeta = np.exp(lane_g * (chunk - 1.0 - idx[:, None]))
    head = np.arange(GROUP) // HEAD_DIM
    cd = (head[:, None] == head[None, :]) * np.exp(lane_g * chunk)
    consts = [jnp.asarray(a, F32) for a in (intra, xi, zeta, cd)]
    row = pl.BlockSpec((rows_per_step, GROUP), lambda b, i: (b * steps + i, 0))
    return pl.pallas_call(
        functools.partial(_ret_body, n_chunks=rows_per_step // chunk),
        grid=(batch, steps),
        in_specs=[row] * 4 + [_const_spec(c.shape) for c in consts]
                 + [_const_spec((GROUP, GROUP)), _const_spec((N_HEADS * chunk, GROUP)), _const_spec((1, GROUP))],
        out_specs=row,
        out_shape=jax.ShapeDtypeStruct((t, GROUP), BF16),
        scratch_shapes=[pltpu.VMEM((GROUP, GROUP), F32)],
        compiler_params=_params("parallel", "arbitrary"),
        name="retention",
    )(q, k, v, gate, *consts, _block_diag(BF16), _stack_mask_const(chunk), norm.reshape(1, GROUP))


SUBLANES = 8


def _boundary_rows(b, m, row):
    if 2 * m >= SUBLANES:
        pieces = []
        for j in range(CHUNK // (2 * m)):
            r = (2 * j + 1) * m - 1
            pieces.append(jnp.broadcast_to(b[r:r + 1, :], (2 * m, GROUP)))
        return jnp.concatenate(pieces, axis=0)
    tiles = b.reshape(CHUNK // SUBLANES, SUBLANES, GROUP)
    sub = (row & (SUBLANES - 1)).reshape(tiles.shape)
    out = None
    for j in range(SUBLANES // (2 * m) - 1, -1, -1):
        r = (2 * j + 1) * m - 1
        piece = jnp.broadcast_to(tiles[:, r:r + 1, :], tiles.shape)
        out = piece if out is None else jnp.where(sub < (2 * j + 2) * m, piece, out)
    return out.reshape(CHUNK, GROUP)


def _gla_body(q_ref, k_ref, v_ref, a_ref, g_ref, ltri_ref, lmask_ref, gmat_ref, bd_ref, qmask_ref, kmask_ref, norm_ref,
              o_ref, state_ref, *, n_chunks):
    @pl.when(pl.program_id(1) == 0)
    def _():
        state_ref[...] = jnp.zeros_like(state_ref)

    masks = _head_masks((CHUNK, GROUP))
    row = lax.broadcasted_iota(jnp.int32, (CHUNK, GROUP), 0)
    gmat = gmat_ref[...]
    ltri = ltri_ref[...]
    for c in range(n_chunks):
        rows = slice(c * CHUNK, (c + 1) * CHUNK)
        q = q_ref[rows, :]
        k = k_ref[rows, :]
        vb = v_ref[rows, :]
        v = vb.astype(F32)
        a = a_ref[rows, :]
        a0 = a.astype(BF16)
        r1 = a - a0.astype(F32)
        a1 = r1.astype(BF16)
        a2 = (r1 - a1.astype(F32)).astype(BF16)
        b = _dot(ltri, a0) + _dot(ltri, a1) + _dot(ltri, a2)
        b_last = b[CHUNK - 1:CHUNK, :]

        state = state_ref[...]
        o = lax.dot_general((q * jnp.exp(b)).astype(BF16), state.astype(BF16), NT_DIMS, preferred_element_type=F32)
        kd = (k * jnp.exp(b_last - b)).astype(BF16)
        kv = lax.dot_general(vb, kd, TN_DIMS, preferred_element_type=F32)
        state_ref[...] = state * jnp.exp(b_last) + kv * bd_ref[...]

        sc = None
        for li, m in enumerate(GLA_LEVELS):
            bref = _boundary_rows(b, m, row)
            upper = ((row >> int(math.log2(m))) & 1) == 1
            w = jnp.exp(-jnp.abs(b - bref))
            scaled = (jnp.where(upper, q, k) * w).astype(BF16)
            qs = _stack_heads(scaled, qmask_ref[li])
            kt = scaled * kmask_ref[li]
            term = lax.dot_general(qs, kt, NT_DIMS, preferred_element_type=F32)
            if 2 * m < CHUNK:
                term = term * lmask_ref[li]
            sc = term if sc is None else sc + term
        o = o + _unstack_heads(_dot(sc.astype(BF16), vb), masks, CHUNK)

        qk = q * k
        qk_hi = qk.astype(BF16)
        prods = [qk_hi, (qk - qk_hi.astype(F32)).astype(BF16)]
        for d in range(1, LEAF):
            valid = (row & (LEAF - 1)) >= d
            w = jnp.exp(jnp.where(valid, b - pltpu.roll(b, d, 0), NEG))
            prods.append((q * pltpu.roll(k, d, 0) * w).astype(BF16))
        sums = _dot(jnp.concatenate(prods, axis=0), gmat)
        o = o + (sums[0:CHUNK] + sums[CHUNK:2 * CHUNK]) * v
        for d in range(1, LEAF):
            o = o + sums[(d + 1) * CHUNK:(d + 2) * CHUNK] * pltpu.roll(v, d, 0)

        o_ref[rows, :] = _head_norm_gate(o, g_ref[rows, :], gmat, norm_ref[...])


def _gla(q, k, v, log_a, gate, norm, batch, rows_per_step):
    t = q.shape[0]
    seq = t // batch
    steps = seq // rows_per_step
    idx = np.arange(CHUNK)
    ltri = jnp.asarray(idx[:, None] >= idx[None, :], BF16)
    lm = []
    for m in GLA_LEVELS:
        same = (idx[:, None] // (2 * m)) == (idx[None, :] // (2 * m))
        ok = same & (((idx[:, None] // m) & 1) == 1) & (((idx[None, :] // m) & 1) == 0)
        lm.append(np.tile(ok, (N_HEADS, 1)))
    lmask = jnp.asarray(np.stack(lm), F32)
    upper = np.stack([((idx // m) & 1) == 1 for m in GLA_LEVELS]).astype(np.float32)
    head = np.arange(GROUP) // HEAD_DIM
    stack = (np.repeat(np.arange(N_HEADS), CHUNK)[:, None] == head[None, :]).astype(np.float32)
    qmask = jnp.asarray(np.tile(upper, (1, N_HEADS))[:, :, None] * stack[None], BF16)
    kmask = jnp.asarray(np.broadcast_to((1.0 - upper)[:, :, None], (len(GLA_LEVELS), CHUNK, GROUP)), BF16)
    row = pl.BlockSpec((rows_per_step, GROUP), lambda b, i: (b * steps + i, 0))
    return pl.pallas_call(
        functools.partial(_gla_body, n_chunks=rows_per_step // CHUNK),
        grid=(batch, steps),
        in_specs=[row] * 5 + [_const_spec((CHUNK, CHUNK)), _const_spec(lmask.shape), _const_spec((GROUP, GROUP)),
                              _const_spec((GROUP, GROUP)), _const_spec(qmask.shape), _const_spec(kmask.shape),
                              _const_spec((1, GROUP))],
        out_specs=row,
        out_shape=jax.ShapeDtypeStruct((t, GROUP), BF16),
        scratch_shapes=[pltpu.VMEM((GROUP, GROUP), F32)],
        compiler_params=_params("parallel", "arbitrary"),
        name="gla",
    )(q, k, v, log_a, gate, ltri, lmask, _block_diag(BF16), _block_diag(F32), qmask, kmask, norm.reshape(1, GROUP))


def _mla_body(q_ref, k_ref, v_ref, o_ref, s0_ref, s1_ref, bm0_ref, bm1_ref, m_ref, acc_ref, *, tq, tk):
    i = pl.program_id(2)
    r = tq // tk
    q = q_ref[...]
    slots = ((s0_ref, bm0_ref), (s1_ref, bm1_ref))

    def scores(blk):
        start = pl.multiple_of(blk * tk, tk)
        return _dot(q, k_ref[:, pl.ds(start, tk)])

    def stash(slot, s):
        s_ref, bm_ref = slots[slot]
        s_ref[...] = s
        bm_ref[...] = jnp.broadcast_to(jnp.max(s, axis=-1, keepdims=True), bm_ref.shape)

    def absorb(slot, blk):
        s_ref, bm_ref = slots[slot]
        start = pl.multiple_of(blk * tk, tk)
        m_prev = m_ref[...]
        m_new = jnp.maximum(m_prev, bm_ref[...])
        m_wide = jnp.concatenate([m_new] * (tk // LANE), axis=1)
        p = jnp.exp2(s_ref[...] - m_wide).astype(BF16)
        alpha = jnp.exp2(m_prev - m_new)
        alpha_wide = jnp.concatenate([alpha] * (MLA_SLOT // LANE), axis=1)
        acc_ref[...] = acc_ref[...] * alpha_wide + _dot(p, v_ref[pl.ds(start, tk), :])
        m_ref[...] = m_new

    def block_at(n):
        return jnp.where(n < r, i * r + n, i * r + (r - 1) - n)

    m_ref[...] = jnp.full(m_ref.shape, NEG, F32)
    acc_ref[...] = jnp.zeros_like(acc_ref)
    row = lax.broadcasted_iota(jnp.int32, (tq, tk), 0)
    col = lax.broadcasted_iota(jnp.int32, (tq, tk), 1)
    for n in range(r):
        stash(n & 1, jnp.where(col + n * tk <= row, scores(i * r + n), NEG))
        if n > 0:
            absorb((n - 1) & 1, i * r + n - 1)

    def body(n, carry):
        for slot in (0, 1):
            @pl.when((n & 1) == slot)
            def _():
                stash(slot, scores(block_at(n)))
                absorb(1 - slot, block_at(n - 1))
        return carry

    steps = (i + 1) * r
    lax.fori_loop(r, steps, body, 0)
    for slot in (0, 1):
        @pl.when(((steps - 1) & 1) == slot)
        def _():
            absorb(slot, block_at(steps - 1))
    acc = acc_ref[...]
    o_ref[...] = (acc[:, :MLA_DV] / acc[:, MLA_DV:MLA_DV + 1]).astype(BF16)


def _mla_attention(q, k, v, batch, tq, tk):
    t = q.shape[0]
    seq = t // batch
    nq = seq // tq
    q3, v3 = (a.reshape(batch, seq, N_HEADS * MLA_SLOT) for a in (q, v))
    k_spec = pl.BlockSpec((None, None, MLA_SLOT, seq), lambda b, h, i: (b, h, 0, 0))
    v_spec = pl.BlockSpec((None, seq, MLA_SLOT), lambda b, h, i: (b, 0, h))
    out = pl.pallas_call(
        functools.partial(_mla_body, tq=tq, tk=tk),
        grid=(batch, N_HEADS, nq),
        in_specs=[pl.BlockSpec((None, tq, MLA_SLOT), lambda b, h, i: (b, i, h)), k_spec, v_spec],
        out_specs=pl.BlockSpec((None, tq, MLA_DV), lambda b, h, i: (b, i, h)),
        out_shape=jax.ShapeDtypeStruct((batch, seq, N_HEADS * MLA_DV), BF16),
        scratch_shapes=[pltpu.VMEM((tq, tk), F32), pltpu.VMEM((tq, tk), F32), pltpu.VMEM((tq, LANE), F32),
                        pltpu.VMEM((tq, LANE), F32), pltpu.VMEM((tq, LANE), F32), pltpu.VMEM((tq, MLA_SLOT), F32)],
        compiler_params=_params("parallel", "parallel", "arbitrary"),
        name="mla_attention",
    )(q3, k, v3)
    return out.reshape(t, N_HEADS * MLA_DV)


def _tiles(batch, seq):
    tm = min(512, seq)
    tm_ffn = min(1024, seq)
    rows_per_step = min(1024, seq)
    tk = min(1024, seq)
    tq = tk
    return tm, tm_ffn, rows_per_step, tq, tk


def kernel(x, positions, ffn1_norm, ffn1_w_gate_up, ffn1_w_down, mix_norm, w_in, ret_out_norm, mla_q_norm, mla_w_uq, mla_kv_norm, mla_w_ukv, mla_q_nope_norm, mla_q_rope_norm, mla_k_nope_norm, mla_k_rope_norm, gla_w_gate_up, gla_gate_bias, gla_out_norm, w_out, ffn2_norm, ffn2_w_gate_up, ffn2_w_down):
    batch, seq, d_model = x.shape
    depth = w_in.shape[0]
    tm, tm_ffn, rows_per_step, tq, tk = _tiles(batch, seq)
    xt = x.reshape(batch * seq, d_model)
    cos, sin = _rope_tables(positions, tm)
    for l in range(depth):
        xt = _ffn(xt, ffn1_norm[l], _layer_bf16(ffn1_w_gate_up, l), _layer_bf16(ffn1_w_down, l), tm_ffn)
        (rq, rk, rv, rg, mq, mk, mv, gq, gk, gv, la, gg) = _inproj(
            xt, cos, sin, mix_norm[l], w_in[l], mla_q_norm[l], mla_w_uq[l], mla_kv_norm[l], mla_w_ukv[l],
            mla_q_nope_norm[l], mla_q_rope_norm[l], mla_k_nope_norm[l], mla_k_rope_norm[l],
            gla_w_gate_up[l], gla_gate_bias[l], batch, tm)
        ret = _retention(rq, rk, rv, rg, ret_out_norm[l], batch, rows_per_step)
        mla = _mla_attention(mq, mk, mv, batch, tq, tk)
        gla = _gla(gq, gk, gv, la, gg, gla_out_norm[l], batch, rows_per_step)
        xt = _ffn(xt, ffn2_norm[l], _layer_bf16(ffn2_w_gate_up, l), _layer_bf16(ffn2_w_down, l), tm_ffn,
                  mix=(ret, mla, gla, w_out[l]))
    return xt.reshape(batch, seq, d_model)
```

```python
import functools
import math

import numpy as np
import jax
import jax.numpy as jnp
from jax import lax
from jax.experimental import pallas as pl
from jax.experimental.pallas import tpu as pltpu

F32 = jnp.float32
BF16 = jnp.bfloat16

EPS = 1e-6
ROPE_THETA = 10000.0
HEAD_DIM = 64
N_HEADS = 4
GROUP = N_HEADS * HEAD_DIM
MLA_NOPE = 128
MLA_ROPE = 64
MLA_DV = 128
MLA_Q_RANK = 384
MLA_KV_RANK = 256
MLA_SLOT = 256
GLA_GATE_RANK = 16
GLA_TAU = 16.0
CHUNK = 128
RET_CHUNK = 256
GLA_LEVELS = (64, 32, 16, 8, 4, 2, 1)
NEG = -1e30

LANE = 128
MXU_N = 256
VMEM_LIMIT_BYTES = 56 * 1024 * 1024

NT_DIMS = (((1,), (1,)), ((), ()))
TN_DIMS = (((0,), (0,)), ((), ()))


def _params(*sem):
    return pltpu.CompilerParams(dimension_semantics=sem, vmem_limit_bytes=VMEM_LIMIT_BYTES)


def _const_spec(shape):
    nd = len(shape)
    return pl.BlockSpec(shape, lambda *_: (0,) * nd, pipeline_mode=pl.Buffered(1))


def _dot(a, b):
    return jnp.dot(a, b, preferred_element_type=F32)


def _rms(x, g):
    return x * lax.rsqrt(jnp.mean(x * x, axis=-1, keepdims=True) + EPS) * g


def _silu(x):
    return x * jax.nn.sigmoid(x)


WEIGHT_CAST_BLOCKS = 4


def _cast_body(w_ref, o_ref):
    o_ref[...] = w_ref[...].astype(BF16)


def _layer_bf16(stacked, layer):
    _, rows, cols = stacked.shape
    br = rows // WEIGHT_CAST_BLOCKS
    return pl.pallas_call(
        _cast_body,
        grid=(WEIGHT_CAST_BLOCKS,),
        in_specs=[pl.BlockSpec((None, br, cols), lambda i: (layer, i, 0))],
        out_specs=pl.BlockSpec((br, cols), lambda i: (i, 0)),
        out_shape=jax.ShapeDtypeStruct((rows, cols), BF16),
        compiler_params=_params("parallel"),
        name="weight_cast",
    )(stacked)


def _rope_table_body(pos_ref, inv_ref, sign_ref, cos_ref, sin_ref):
    ang = pos_ref[...].astype(F32) * inv_ref[...]
    cos_ref[...] = jnp.cos(ang)
    sin_ref[...] = jnp.sin(ang) * sign_ref[...]


def _rope_tables(positions, tm):
    t = positions.size
    half = HEAD_DIM // 2
    inv = ROPE_THETA ** (-jnp.arange(0, HEAD_DIM, 2, dtype=F32) / HEAD_DIM)
    inv_row = jnp.tile(inv, LANE // half).reshape(1, LANE)
    sign_row = jnp.tile(jnp.concatenate([-jnp.ones(half, F32), jnp.ones(half, F32)]), LANE // HEAD_DIM).reshape(1, LANE)
    pos = positions.reshape(t, 1)
    row = pl.BlockSpec((tm, LANE), lambda i: (i, 0))
    return pl.pallas_call(
        _rope_table_body,
        grid=(t // tm,),
        in_specs=[pl.BlockSpec((tm, 1), lambda i: (i, 0)), _const_spec((1, LANE)), _const_spec((1, LANE))],
        out_specs=[row, row],
        out_shape=[jax.ShapeDtypeStruct((t, LANE), F32)] * 2,
        compiler_params=_params("parallel"),
        name="rope_tables",
    )(pos, inv_row, sign_row)


def _rope(x, cos, sin):
    w = x.shape[-1]
    reps = w // LANE
    if reps > 1:
        cos = jnp.concatenate([cos] * reps, axis=-1)
        sin = jnp.concatenate([sin] * reps, axis=-1)
    lane = lax.broadcasted_iota(jnp.int32, x.shape, 1)
    first_half = (lane & (HEAD_DIM // 2)) == 0
    swapped = jnp.where(first_half, pltpu.roll(x, w - HEAD_DIM // 2, 1), pltpu.roll(x, HEAD_DIM // 2, 1))
    return x * cos + swapped * sin


def _ffn_body(*refs, d_ff, d_model, with_mix):
    if with_mix:
        x_ref, r_ref, m_ref, a_ref, wr_ref, wm_ref, wa_ref, g_ref, wgu_ref, wd_ref, o_ref, act_ref, x_mid_ref = refs
        r, m, a = r_ref[...], m_ref[...], a_ref[...]
        for c in range(d_model // MXU_N):
            cols = slice(c * MXU_N, (c + 1) * MXU_N)
            x_mid_ref[:, cols] = x_ref[:, cols] + (
                _dot(r, wr_ref[:, cols]) + _dot(m, wm_ref[:, cols]) + _dot(a, wa_ref[:, cols]))
        x_ref = x_mid_ref
    else:
        x_ref, g_ref, wgu_ref, wd_ref, o_ref, act_ref = refs
    xn = _rms(x_ref[...], g_ref[...]).astype(BF16)
    for c in range(d_ff // MXU_N):
        cols = slice(c * MXU_N, (c + 1) * MXU_N)
        ucols = slice(d_ff + c * MXU_N, d_ff + (c + 1) * MXU_N)
        g = _dot(xn, wgu_ref[:, cols])
        u = _dot(xn, wgu_ref[:, ucols])
        act_ref[:, cols] = (_silu(g) * u).astype(BF16)
    act = act_ref[...]
    for c in range(d_model // MXU_N):
        cols = slice(c * MXU_N, (c + 1) * MXU_N)
        o_ref[:, cols] = x_ref[:, cols] + 0.5 * _dot(act, wd_ref[:, cols])


def _ffn(x, norm, w_gate_up, w_down, tm, mix=None):
    t, d_model = x.shape
    d_ff = w_down.shape[0]
    row = lambda n: pl.BlockSpec((tm, n), lambda i: (i, 0))
    operands, specs, scratch = [x], [row(d_model)], [pltpu.VMEM((tm, d_ff), BF16)]
    if mix is not None:
        ret, mla, gla, w_out = mix
        nr, nm, na = ret.shape[1], mla.shape[1], gla.shape[1]
        w = w_out.astype(BF16)
        operands += [ret, mla, gla, w[:nr], w[nr:nr + nm], w[nr + nm:]]
        specs += [row(nr), row(nm), row(na), _const_spec((nr, d_model)), _const_spec((nm, d_model)),
                  _const_spec((na, d_model))]
        scratch.append(pltpu.VMEM((tm, d_model), F32))
    operands += [norm.reshape(1, d_model), w_gate_up.astype(BF16), w_down.astype(BF16)]
    specs += [_const_spec((1, d_model)), _const_spec((d_model, 2 * d_ff)), _const_spec((d_ff, d_model))]
    return pl.pallas_call(
        functools.partial(_ffn_body, d_ff=d_ff, d_model=d_model, with_mix=mix is not None),
        grid=(t // tm,),
        in_specs=specs,
        out_specs=row(d_model),
        out_shape=jax.ShapeDtypeStruct((t, d_model), F32),
        scratch_shapes=scratch,
        compiler_params=_params("parallel"),
        name="mix_ffn" if mix is not None else "ffn",
    )(*operands)


W_RET = 0
W_CQ = 4 * GROUP
W_TAIL = W_CQ + MLA_Q_RANK
W_CKV = W_TAIL + LANE
W_GLA = W_CKV + MLA_KV_RANK
W_GLA_R = W_GLA + 3 * GROUP
W_COLS = W_GLA_R + GROUP


def _inproj_body(x_ref, g_ref, cos_ref, sin_ref, w_ref, qn_ref, wuq_ref, kvn_ref, wukv_ref,
                 gqn_ref, gqr_ref, gkn_ref, gkr_ref, wga_ref, ba_ref,
                 rq_ref, rk_ref, rv_ref, rg_ref, mq_ref, mk_ref, mv_ref,
                 gq_ref, gk_ref, gv_ref, la_ref, gg_ref):
    cos = cos_ref[...]
    sin = sin_ref[...]
    xn = _rms(x_ref[...], g_ref[...]).astype(BF16)
    tm = xn.shape[0]
    lane = lax.broadcasted_iota(jnp.int32, (tm, LANE), 1)
    low_half = lane < HEAD_DIM
    dk_scale = HEAD_DIM ** -0.5

    def proj(lo, width):
        return _dot(xn, w_ref[:, lo:lo + width])

    rq_ref[...] = _rope(proj(W_RET, GROUP), cos, sin).astype(BF16)
    rk_ref[...] = (_rope(proj(W_RET + GROUP, GROUP), cos, sin) * dk_scale).astype(BF16)
    rv_ref[...] = proj(W_RET + 2 * GROUP, GROUP).astype(BF16)
    rg_ref[...] = _silu(proj(W_RET + 3 * GROUP, GROUP)).astype(BF16)

    gq_ref[...] = proj(W_GLA, GROUP) * dk_scale
    gk_ref[...] = proj(W_GLA + GROUP, GROUP)
    gv_ref[...] = proj(W_GLA + 2 * GROUP, GROUP).astype(BF16)
    gg_ref[...] = _silu(proj(W_GLA_R, GROUP)).astype(BF16)
    cq_tail = proj(W_CQ, MLA_Q_RANK + LANE)
    tail = cq_tail[:, MLA_Q_RANK:]
    z = _dot(tail.astype(BF16), wga_ref[...]) + ba_ref[...]
    la_ref[...] = (jnp.minimum(z, 0.0) - jnp.log(1.0 + jnp.exp(-jnp.abs(z)))) * (1.0 / GLA_TAU)

    kr_ms = jnp.sum(jnp.where(low_half, tail * tail, 0.0), axis=-1, keepdims=True) * (1.0 / MLA_ROPE)
    kr = _rope(tail * lax.rsqrt(kr_ms + EPS) * gkr_ref[...], cos, sin)
    kr_t = kr.T.astype(BF16)
    ckv = _rms(proj(W_CKV, MLA_KV_RANK), kvn_ref[...]).astype(BF16)
    ones_col = jnp.where(lane == 0, 1.0, 0.0).astype(BF16)
    v_base = N_HEADS * MLA_NOPE
    for pair in range(N_HEADS // 2):
        kn2 = _dot(ckv, wukv_ref[:, pair * MXU_N:(pair + 1) * MXU_N])
        v2 = _dot(ckv, wukv_ref[:, v_base + pair * MXU_N:v_base + (pair + 1) * MXU_N])
        for sub in range(2):
            h = 2 * pair + sub
            kn = kn2[:, sub * MLA_NOPE:(sub + 1) * MLA_NOPE]
            mk_ref[h, 0:MLA_NOPE, :] = _rms(kn, gkn_ref[...]).T.astype(BF16)
            mk_ref[h, MLA_NOPE:MLA_SLOT, :] = kr_t
            mv_ref[:, h * MLA_SLOT:h * MLA_SLOT + MLA_DV] = v2[:, sub * MLA_DV:(sub + 1) * MLA_DV].astype(BF16)
            mv_ref[:, h * MLA_SLOT + MLA_DV:(h + 1) * MLA_SLOT] = ones_col

    sm_scale = (MLA_NOPE + MLA_ROPE) ** -0.5 * math.log2(math.e)
    cq = _rms(cq_tail[:, :MLA_Q_RANK], qn_ref[...]).astype(BF16)
    for pair in range(N_HEADS // 2):
        qn2 = _dot(cq, wuq_ref[:, pair * MXU_N:(pair + 1) * MXU_N])
        for sub in range(2):
            h = 2 * pair + sub
            qn = qn2[:, sub * MLA_NOPE:(sub + 1) * MLA_NOPE]
            mq_ref[:, h * MLA_SLOT:h * MLA_SLOT + MLA_NOPE] = (_rms(qn, gqn_ref[...]) * sm_scale).astype(BF16)
    qr4 = _dot(cq, wuq_ref[:, N_HEADS * MLA_NOPE:N_HEADS * MLA_NOPE + N_HEADS * MLA_ROPE])
    for pair in range(N_HEADS // 2):
        qr = qr4[:, pair * LANE:(pair + 1) * LANE]
        q2 = qr * qr
        ms_lo = jnp.sum(jnp.where(low_half, q2, 0.0), axis=-1, keepdims=True)
        ms_hi = jnp.sum(jnp.where(low_half, 0.0, q2), axis=-1, keepdims=True)
        ms = jnp.where(low_half, ms_lo, ms_hi) * (1.0 / MLA_ROPE)
        qr = _rope(qr * lax.rsqrt(ms + EPS) * gqr_ref[...], cos, sin) * sm_scale
        h0 = 2 * pair
        mq_ref[:, h0 * MLA_SLOT + MLA_NOPE:(h0 + 1) * MLA_SLOT] = jnp.where(low_half, qr, 0.0).astype(BF16)
        mq_ref[:, (h0 + 1) * MLA_SLOT + MLA_NOPE:(h0 + 2) * MLA_SLOT] = jnp.where(
            low_half, pltpu.roll(qr, HEAD_DIM, 1), 0.0).astype(BF16)


def _reorder_w_in(w_in):
    d = w_in.shape[0]
    o = np.cumsum([0, GROUP, GROUP, GROUP, GROUP, MLA_Q_RANK, MLA_KV_RANK, MLA_ROPE,
                   GROUP, GROUP, GROUP, GLA_GATE_RANK, GROUP])
    seg = [w_in[:, o[i]:o[i + 1]] for i in range(12)]
    ret_q, ret_k, ret_v, ret_g, cq, ckv, kr, gq, gk, gv, a_low, gr = seg
    pad = jnp.zeros((d, LANE - MLA_ROPE - GLA_GATE_RANK), w_in.dtype)
    return jnp.concatenate([ret_q, ret_k, ret_v, ret_g, cq, kr, a_low, pad, ckv, gq, gk, gv, gr], axis=1)


def _inproj(x, cos, sin, norm, w_in, mla_q_norm, mla_w_uq, mla_kv_norm, mla_w_ukv,
            q_nope_norm, q_rope_norm, k_nope_norm, k_rope_norm, gla_w_gate_up, gla_gate_bias, batch, tm):
    t, d = x.shape
    seq = t // batch
    tiles_per_seq = seq // tm
    w = _reorder_w_in(w_in).astype(BF16)
    qd = MLA_NOPE + MLA_ROPE
    wuq = mla_w_uq.reshape(MLA_Q_RANK, N_HEADS, qd)
    wuq = jnp.concatenate([wuq[:, :, :MLA_NOPE].reshape(MLA_Q_RANK, -1),
                           wuq[:, :, MLA_NOPE:].reshape(MLA_Q_RANK, -1)], axis=1).astype(BF16)
    wukv = mla_w_ukv.reshape(MLA_KV_RANK, N_HEADS, MLA_NOPE + MLA_DV)
    wukv = jnp.concatenate([wukv[:, :, :MLA_NOPE].reshape(MLA_KV_RANK, -1),
                            wukv[:, :, MLA_NOPE:].reshape(MLA_KV_RANK, -1)], axis=1).astype(BF16)
    gqr = jnp.tile(q_rope_norm, LANE // MLA_ROPE).reshape(1, LANE)
    gkr = jnp.concatenate([k_rope_norm, jnp.zeros(LANE - MLA_ROPE, F32)]).reshape(1, LANE)
    wga = jnp.zeros((LANE, GROUP), F32).at[MLA_ROPE:MLA_ROPE + GLA_GATE_RANK].set(gla_w_gate_up).astype(BF16)

    row = lambda n: pl.BlockSpec((tm, n), lambda i: (i, 0))
    out = lambda n, dt: jax.ShapeDtypeStruct((t, n), dt)
    wide = N_HEADS * MLA_SLOT
    keys_t = pl.BlockSpec((None, N_HEADS, MLA_SLOT, tm), lambda i: (i // tiles_per_seq, 0, 0, i % tiles_per_seq))
    return pl.pallas_call(
        _inproj_body,
        grid=(t // tm,),
        in_specs=[row(d), _const_spec((1, d)), row(LANE), row(LANE), _const_spec((d, W_COLS)),
                  _const_spec((1, MLA_Q_RANK)), _const_spec(wuq.shape), _const_spec((1, MLA_KV_RANK)),
                  _const_spec(wukv.shape), _const_spec((1, MLA_NOPE)), _const_spec((1, LANE)),
                  _const_spec((1, MLA_NOPE)), _const_spec((1, LANE)), _const_spec((LANE, GROUP)),
                  _const_spec((1, GROUP))],
        out_specs=[row(GROUP)] * 4 + [row(wide), keys_t, row(wide)] + [row(GROUP)] * 5,
        out_shape=[out(GROUP, BF16)] * 4
                  + [out(wide, BF16), jax.ShapeDtypeStruct((batch, N_HEADS, MLA_SLOT, seq), BF16), out(wide, BF16)]
                  + [out(GROUP, F32), out(GROUP, F32), out(GROUP, BF16), out(GROUP, F32), out(GROUP, BF16)],
        compiler_params=_params("parallel"),
        name="inproj",
    )(x, norm.reshape(1, d), cos, sin, w, mla_q_norm.reshape(1, -1), wuq, mla_kv_norm.reshape(1, -1), wukv,
      q_nope_norm.reshape(1, -1), gqr, k_nope_norm.reshape(1, -1), gkr, wga, gla_gate_bias.reshape(1, -1))


def _head_masks(shape):
    lane = lax.broadcasted_iota(jnp.int32, shape, 1)
    return [(lane >> int(math.log2(HEAD_DIM))) == h for h in range(N_HEADS)]


def _stack_heads(x, stack_mask):
    return jnp.concatenate([x] * N_HEADS, axis=0) * stack_mask


def _unstack_heads(y, masks, c):
    out = y[(N_HEADS - 1) * c:N_HEADS * c]
    for h in range(N_HEADS - 2, -1, -1):
        out = jnp.where(masks[h], y[h * c:(h + 1) * c], out)
    return out


def _stack_mask_const(chunk):
    head = np.arange(GROUP) // HEAD_DIM
    block = np.repeat(np.arange(N_HEADS), chunk)
    return jnp.asarray((block[:, None] == head[None, :]).astype(np.float32), BF16)


def _head_norm_gate(o, gate, gmat, norm):
    ms = _dot((o * o).astype(BF16), gmat) * (1.0 / HEAD_DIM)
    return (gate.astype(F32) * (o * lax.rsqrt(ms + EPS) * norm)).astype(BF16)


def _block_diag(dtype):
    idx = np.arange(GROUP) // HEAD_DIM
    return jnp.asarray((idx[:, None] == idx[None, :]).astype(np.float32), dtype)


def _ret_body(q_ref, k_ref, v_ref, g_ref, intra_ref, xi_ref, zeta_ref, cd_ref, gmat_ref, smask_ref, norm_ref,
              o_ref, state_ref, *, n_chunks):
    @pl.when(pl.program_id(1) == 0)
    def _():
        state_ref[...] = jnp.zeros_like(state_ref)

    masks = _head_masks((RET_CHUNK, GROUP))
    for c in range(n_chunks):
        rows = slice(c * RET_CHUNK, (c + 1) * RET_CHUNK)
        q = q_ref[rows, :]
        k = k_ref[rows, :]
        v = v_ref[rows, :]
        sc = lax.dot_general(_stack_heads(q, smask_ref[...]), k, NT_DIMS, preferred_element_type=F32) * intra_ref[...]
        o = _unstack_heads(_dot(sc.astype(BF16), v), masks, RET_CHUNK)
        state = state_ref[...]
        o = o + _dot(q, state.astype(BF16)) * xi_ref[...]
        vz = (v.astype(F32) * zeta_ref[...]).astype(BF16)
        kv = lax.dot_general(k, vz, TN_DIMS, preferred_element_type=F32)
        state_ref[...] = state * cd_ref[...] + kv * (cd_ref[...] > 0.0).astype(F32)
        o_ref[rows, :] = _head_norm_gate(o, g_ref[rows, :], gmat_ref[...], norm_ref[...])


def _retention(q, k, v, gate, norm, batch, rows_per_step):
    t = q.shape[0]
    seq = t // batch
    steps = seq // rows_per_step
    log_g = np.log1p(-np.power(2.0, -5.0 - np.arange(N_HEADS, dtype=np.float64)))
    chunk = RET_CHUNK
    idx = np.arange(chunk, dtype=np.float64)
    diff = idx[:, None] - idx[None, :]
    intra = np.where(diff >= 0, np.exp(log_g[:, None, None] * np.maximum(diff, 0.0)), 0.0)
    intra = intra.reshape(N_HEADS * chunk, chunk)
    lane_g = np.repeat(log_g, HEAD_DIM)[None, :]
    xi = np.exp(lane_g * (idx[:, None] + 1.0))
    zeta = np.exp(lane_g * (chunk - 1.0 - idx[:, None]))
    head = np.arange(GROUP) // HEAD_DIM
    cd = (head[:, None] == head[None, :]) * np.exp(lane_g * chunk)
    consts = [jnp.asarray(a, F32) for a in (intra, xi, zeta, cd)]
    row = pl.BlockSpec((rows_per_step, GROUP), lambda b, i: (b * steps + i, 0))
    return pl.pallas_call(
        functools.partial(_ret_body, n_chunks=rows_per_step // chunk),
        grid=(batch, steps),
        in_specs=[row] * 4 + [_const_spec(c.shape) for c in consts]
                 + [_const_spec((GROUP, GROUP)), _const_spec((N_HEADS * chunk, GROUP)), _const_spec((1, GROUP))],
        out_specs=row,
        out_shape=jax.ShapeDtypeStruct((t, GROUP), BF16),
        scratch_shapes=[pltpu.VMEM((GROUP, GROUP), F32)],
        compiler_params=_params("parallel", "arbitrary"),
        name="retention",
    )(q, k, v, gate, *consts, _block_diag(BF16), _stack_mask_const(chunk), norm.reshape(1, GROUP))


SUBLANES = 8


def _boundary_rows(b, m, row):
    if 2 * m >= SUBLANES:
        pieces = []
        for j in range(CHUNK // (2 * m)):
            r = (2 * j + 1) * m - 1
            pieces.append(jnp.broadcast_to(b[r:r + 1, :], (2 * m, GROUP)))
        return jnp.concatenate(pieces, axis=0)
    tiles = b.reshape(CHUNK // SUBLANES, SUBLANES, GROUP)
    sub = (row & (SUBLANES - 1)).reshape(tiles.shape)
    out = None
    for j in range(SUBLANES // (2 * m) - 1, -1, -1):
        r = (2 * j + 1) * m - 1
        piece = jnp.broadcast_to(tiles[:, r:r + 1, :], tiles.shape)
        out = piece if out is None else jnp.where(sub < (2 * j + 2) * m, piece, out)
    return out.reshape(CHUNK, GROUP)


def _gla_body(q_ref, k_ref, v_ref, a_ref, g_ref, ltri_ref, lmask_ref, gmat_ref, bd_ref, qmask_ref, kmask_ref, norm_ref,
              o_ref, state_ref, *, n_chunks):
    @pl.when(pl.program_id(1) == 0)
    def _():
        state_ref[...] = jnp.zeros_like(state_ref)

    masks = _head_masks((CHUNK, GROUP))
    row = lax.broadcasted_iota(jnp.int32, (CHUNK, GROUP), 0)
    gmat = gmat_ref[...]
    ltri = ltri_ref[...]
    for c in range(n_chunks):
        rows = slice(c * CHUNK, (c + 1) * CHUNK)
        q = q_ref[rows, :]
        k = k_ref[rows, :]
        vb = v_ref[rows, :]
        v = vb.astype(F32)
        a = a_ref[rows, :]
        a0 = a.astype(BF16)
        r1 = a - a0.astype(F32)
        a1 = r1.astype(BF16)
        a2 = (r1 - a1.astype(F32)).astype(BF16)
        b = _dot(ltri, a0) + _dot(ltri, a1) + _dot(ltri, a2)
        b_last = b[CHUNK - 1:CHUNK, :]

        state = state_ref[...]
        o = lax.dot_general((q * jnp.exp(b)).astype(BF16), state.astype(BF16), NT_DIMS, preferred_element_type=F32)
        kd = (k * jnp.exp(b_last - b)).astype(BF16)
        kv = lax.dot_general(vb, kd, TN_DIMS, preferred_element_type=F32)
        state_ref[...] = state * jnp.exp(b_last) + kv * bd_ref[...]

        sc = None
        for li, m in enumerate(GLA_LEVELS):
            bref = _boundary_rows(b, m, row)
            upper = ((row >> int(math.log2(m))) & 1) == 1
            w = jnp.exp(-jnp.abs(b - bref))
            scaled = (jnp.where(upper, q, k) * w).astype(BF16)
            qs = _stack_heads(scaled, qmask_ref[li])
            kt = scaled * kmask_ref[li]
            term = lax.dot_general(qs, kt, NT_DIMS, preferred_element_type=F32)
            if 2 * m < CHUNK:
                term = term * lmask_ref[li]
            sc = term if sc is None else sc + term
        o = o + _unstack_heads(_dot(sc.astype(BF16), vb), masks, CHUNK)

        qk = q * k
        qk_hi = qk.astype(BF16)
        sums = _dot(jnp.concatenate([qk_hi, (qk - qk_hi.astype(F32)).astype(BF16)], axis=0), gmat)
        o = o + (sums[0:CHUNK] + sums[CHUNK:2 * CHUNK]) * v

        o_ref[rows, :] = _head_norm_gate(o, g_ref[rows, :], gmat, norm_ref[...])


def _gla(q, k, v, log_a, gate, norm, batch, rows_per_step):
    t = q.shape[0]
    seq = t // batch
    steps = seq // rows_per_step
    idx = np.arange(CHUNK)
    ltri = jnp.asarray(idx[:, None] >= idx[None, :], BF16)
    lm = []
    for m in GLA_LEVELS:
        same = (idx[:, None] // (2 * m)) == (idx[None, :] // (2 * m))
        ok = same & (((idx[:, None] // m) & 1) == 1) & (((idx[None, :] // m) & 1) == 0)
        lm.append(np.tile(ok, (N_HEADS, 1)))
    lmask = jnp.asarray(np.stack(lm), F32)
    upper = np.stack([((idx // m) & 1) == 1 for m in GLA_LEVELS]).astype(np.float32)
    head = np.arange(GROUP) // HEAD_DIM
    stack = (np.repeat(np.arange(N_HEADS), CHUNK)[:, None] == head[None, :]).astype(np.float32)
    qmask = jnp.asarray(np.tile(upper, (1, N_HEADS))[:, :, None] * stack[None], BF16)
    kmask = jnp.asarray(np.broadcast_to((1.0 - upper)[:, :, None], (len(GLA_LEVELS), CHUNK, GROUP)), BF16)
    row = pl.BlockSpec((rows_per_step, GROUP), lambda b, i: (b * steps + i, 0))
    return pl.pallas_call(
        functools.partial(_gla_body, n_chunks=rows_per_step // CHUNK),
        grid=(batch, steps),
        in_specs=[row] * 5 + [_const_spec((CHUNK, CHUNK)), _const_spec(lmask.shape), _const_spec((GROUP, GROUP)),
                              _const_spec((GROUP, GROUP)), _const_spec(qmask.shape), _const_spec(kmask.shape),
                              _const_spec((1, GROUP))],
        out_specs=row,
        out_shape=jax.ShapeDtypeStruct((t, GROUP), BF16),
        scratch_shapes=[pltpu.VMEM((GROUP, GROUP), F32)],
        compiler_params=_params("parallel", "arbitrary"),
        name="gla",
    )(q, k, v, log_a, gate, ltri, lmask, _block_diag(BF16), _block_diag(F32), qmask, kmask, norm.reshape(1, GROUP))


def _mla_body(q_ref, k_ref, v_ref, o_ref, s0_ref, s1_ref, bm0_ref, bm1_ref, m_ref, acc_ref, *, tq, tk):
    i = pl.program_id(2)
    r = tq // tk
    q = q_ref[...]
    slots = ((s0_ref, bm0_ref), (s1_ref, bm1_ref))

    def scores(blk):
        start = pl.multiple_of(blk * tk, tk)
        return _dot(q, k_ref[:, pl.ds(start, tk)])

    def stash(slot, s):
        s_ref, bm_ref = slots[slot]
        s_ref[...] = s
        bm_ref[...] = jnp.broadcast_to(jnp.max(s, axis=-1, keepdims=True), bm_ref.shape)

    def absorb(slot, blk):
        s_ref, bm_ref = slots[slot]
        start = pl.multiple_of(blk * tk, tk)
        m_prev = m_ref[...]
        m_new = jnp.maximum(m_prev, bm_ref[...])
        m_wide = jnp.concatenate([m_new] * (tk // LANE), axis=1)
        p = jnp.exp2(s_ref[...] - m_wide).astype(BF16)
        alpha = jnp.exp2(m_prev - m_new)
        alpha_wide = jnp.concatenate([alpha] * (MLA_SLOT // LANE), axis=1)
        acc_ref[...] = acc_ref[...] * alpha_wide + _dot(p, v_ref[pl.ds(start, tk), :])
        m_ref[...] = m_new

    def block_at(n):
        return jnp.where(n < r, i * r + n, i * r + (r - 1) - n)

    m_ref[...] = jnp.full(m_ref.shape, NEG, F32)
    acc_ref[...] = jnp.zeros_like(acc_ref)
    row = lax.broadcasted_iota(jnp.int32, (tq, tk), 0)
    col = lax.broadcasted_iota(jnp.int32, (tq, tk), 1)
    for n in range(r):
        stash(n & 1, jnp.where(col + n * tk <= row, scores(i * r + n), NEG))
        if n > 0:
            absorb((n - 1) & 1, i * r + n - 1)

    def body(n, carry):
        for slot in (0, 1):
            @pl.when((n & 1) == slot)
            def _():
                stash(slot, scores(block_at(n)))
                absorb(1 - slot, block_at(n - 1))
        return carry

    steps = (i + 1) * r
    lax.fori_loop(r, steps, body, 0)
    for slot in (0, 1):
        @pl.when(((steps - 1) & 1) == slot)
        def _():
            absorb(slot, block_at(steps - 1))
    acc = acc_ref[...]
    o_ref[...] = (acc[:, :MLA_DV] / acc[:, MLA_DV:MLA_DV + 1]).astype(BF16)


def _mla_attention(q, k, v, batch, tq, tk):
    t = q.shape[0]
    seq = t // batch
    nq = seq // tq
    q3, v3 = (a.reshape(batch, seq, N_HEADS * MLA_SLOT) for a in (q, v))
    k_spec = pl.BlockSpec((None, None, MLA_SLOT, seq), lambda b, h, i: (b, h, 0, 0))
    v_spec = pl.BlockSpec((None, seq, MLA_SLOT), lambda b, h, i: (b, 0, h))
    out = pl.pallas_call(
        functools.partial(_mla_body, tq=tq, tk=tk),
        grid=(batch, N_HEADS, nq),
        in_specs=[pl.BlockSpec((None, tq, MLA_SLOT), lambda b, h, i: (b, i, h)), k_spec, v_spec],
        out_specs=pl.BlockSpec((None, tq, MLA_DV), lambda b, h, i: (b, i, h)),
        out_shape=jax.ShapeDtypeStruct((batch, seq, N_HEADS * MLA_DV), BF16),
        scratch_shapes=[pltpu.VMEM((tq, tk), F32), pltpu.VMEM((tq, tk), F32), pltpu.VMEM((tq, LANE), F32),
                        pltpu.VMEM((tq, LANE), F32), pltpu.VMEM((tq, LANE), F32), pltpu.VMEM((tq, MLA_SLOT), F32)],
        compiler_params=_params("parallel", "parallel", "arbitrary"),
        name="mla_attention",
    )(q3, k, v3)
    return out.reshape(t, N_HEADS * MLA_DV)


def _tiles(batch, seq):
    tm = min(512, seq)
    tm_ffn = min(1024, seq)
    rows_per_step = min(1024, seq)
    tk = min(1024, seq)
    tq = tk
    return tm, tm_ffn, rows_per_step, tq, tk


def kernel(x, positions, ffn1_norm, ffn1_w_gate_up, ffn1_w_down, mix_norm, w_in, ret_out_norm, mla_q_norm, mla_w_uq, mla_kv_norm, mla_w_ukv, mla_q_nope_norm, mla_q_rope_norm, mla_k_nope_norm, mla_k_rope_norm, gla_w_gate_up, gla_gate_bias, gla_out_norm, w_out, ffn2_norm, ffn2_w_gate_up, ffn2_w_down):
    batch, seq, d_model = x.shape
    depth = w_in.shape[0]
    tm, tm_ffn, rows_per_step, tq, tk = _tiles(batch, seq)
    xt = x.reshape(batch * seq, d_model)
    cos, sin = _rope_tables(positions, tm)
    for l in range(depth):
        xt = _ffn(xt, ffn1_norm[l], _layer_bf16(ffn1_w_gate_up, l), _layer_bf16(ffn1_w_down, l), tm_ffn)
        (rq, rk, rv, rg, mq, mk, mv, gq, gk, gv, la, gg) = _inproj(
            xt, cos, sin, mix_norm[l], w_in[l], mla_q_norm[l], mla_w_uq[l], mla_kv_norm[l], mla_w_ukv[l],
            mla_q_nope_norm[l], mla_q_rope_norm[l], mla_k_nope_norm[l], mla_k_rope_norm[l],
            gla_w_gate_up[l], gla_gate_bias[l], batch, tm)
        ret = _retention(rq, rk, rv, rg, ret_out_norm[l], batch, rows_per_step)
        mla = _mla_attention(mq, mk, mv, batch, tq, tk)
        gla = _gla(gq, gk, gv, la, gg, gla_out_norm[l], batch, rows_per_step)
        xt = _ffn(xt, ffn2_norm[l], _layer_bf16(ffn2_w_gate_up, l), _layer_bf16(ffn2_w_down, l), tm_ffn,
                  mix=(ret, mla, gla, w_out[l]))
    return xt.reshape(batch, seq, d_model)
```
